```python
import math
import jax
import jax.numpy as jnp
from jax import lax
import numpy as np

D_MODEL = 1024
BATCH = 8
SEQ = 2048
DEPTH = 4

GRID_W = 64
CTX_LEN = 256
N_BRANCH = 3
BRANCH_W = D_MODEL

MLA_HEADS = D_MODEL // 128
QK_NOPE = 128
QK_ROPE = 64
V_DIM = 128
Q_LORA = D_MODEL // 2
KV_LORA = D_MODEL // 4
ROPE_FREQS = QK_ROPE // 4
ROPE_THETA = 10000.0
ATTN_SCALE = (QK_NOPE + QK_ROPE) ** -0.5
ATTN_Q_BLOCK = 128

SSD_HEADDIM = 64
SSD_INNER = D_MODEL
SSD_HEADS = SSD_INNER // SSD_HEADDIM
SSD_GROUPS = 4
SSD_STATE = 128
SSD_CHUNK = 128
SSD_CONV_CH = SSD_INNER + 2 * SSD_GROUPS * SSD_STATE

CONV_W = 4
CONV_PAD_L = CONV_W // 2
CONV_PAD_R = CONV_W - 1 - CONV_PAD_L

LRU_WIDTH = D_MODEL
LRU_BW = 64
LRU_BLOCKS = LRU_WIDTH // LRU_BW
LRU_C = 8.0

N_GROUPS = 4
EXPERTS_PER_GROUP = 8
N_EXPERTS = N_GROUPS * EXPERTS_PER_GROUP
TOP_K = 2
EXPERT_HIDDEN = D_MODEL // 2
EXPERT_BLOCK = 128

ALPHA = (2 * DEPTH) ** 0.25
BETA = (8 * DEPTH) ** -0.25
LN_EPS = 1e-5
RMS_EPS = 1e-6

IN_SECTIONS = (Q_LORA, KV_LORA, QK_ROPE, SSD_INNER, SSD_CONV_CH, 2 * SSD_HEADS, LRU_WIDTH, LRU_WIDTH, N_BRANCH * D_MODEL)
IN_COLS = sum(IN_SECTIONS)

kernel_name = 'hybrid_mla_ssd_rglru_hmoe_diffusion'


def layer_norm(t, gain=None, bias=None):
    tf = t.astype(jnp.float32)
    mu = jnp.mean(tf, -1, keepdims=True)
    var = jnp.mean(jnp.square(tf - mu), -1, keepdims=True)
    y = ((tf - mu) * lax.rsqrt(var + LN_EPS)).astype(t.dtype)
    if gain is None:
        return y
    return y * gain + bias


def rms_norm(t, gain):
    tf = t.astype(jnp.float32)
    y = tf * lax.rsqrt(jnp.mean(jnp.square(tf), -1, keepdims=True) + RMS_EPS)
    return y.astype(t.dtype) * gain


def modulate(t, shift, scale):
    return layer_norm(t) * (1 + scale) + shift


def split_in(p):
    out, start = [], 0
    for size in IN_SECTIONS:
        out.append(p[..., start:start + size])
        start += size
    return out


def dwconv_centred(t, w, bias):
    ch = t.shape[-1]
    y = lax.conv_general_dilated(t, w[:, None, :].astype(t.dtype), window_strides=(1,),
                                 padding=[(CONV_PAD_L, CONV_PAD_R)],
                                 dimension_numbers=('NWC', 'WIO', 'NWC'), feature_group_count=ch)
    return y + bias


def axial_rope(t, cos, sin):
    shp = t.shape
    t = t.reshape(shp[:-1] + (2, 2, ROPE_FREQS))
    t1, t2 = t[..., 0, :], t[..., 1, :]
    out = jnp.stack([t1 * cos - t2 * sin, t2 * cos + t1 * sin], axis=-2)
    return out.reshape(shp)


def mla_q(cq_raw, norm_w, w_uq, rope):
    b, l, _ = cq_raw.shape
    q = (rms_norm(cq_raw, norm_w) @ w_uq).reshape(b, l, MLA_HEADS, QK_NOPE + QK_ROPE)
    qn, qr = q[..., :QK_NOPE], q[..., QK_NOPE:]
    if rope is not None:
        qr = axial_rope(qr, rope[0][:, None], rope[1][:, None])
    return qn, qr


def mla_kv(ckv_raw, kr, norm_w, w_ukv, rope):
    b, l, _ = ckv_raw.shape
    kv = (rms_norm(ckv_raw, norm_w) @ w_ukv).reshape(b, l, MLA_HEADS, QK_NOPE + V_DIM)
    kn, v = kv[..., :QK_NOPE], kv[..., QK_NOPE:]
    if rope is not None:
        kr = axial_rope(kr, rope[0], rope[1])
    return kn, v, kr


def mla_attend(qn, qr, kn, kr, v):
    b, lq, h, _ = qn.shape
    nb = lq // ATTN_Q_BLOCK

    def block(args):
        qnb, qrb = args
        s = jnp.einsum('bqhd,bkhd->bhqk', qnb, kn) + jnp.einsum('bqhr,bkr->bhqk', qrb, kr)
        p = jax.nn.softmax(s.astype(jnp.float32) * ATTN_SCALE, axis=-1).astype(v.dtype)
        return jnp.einsum('bhqk,bkhd->bqhd', p, v)

    qn_b = qn.reshape(b, nb, ATTN_Q_BLOCK, h, QK_NOPE).swapaxes(0, 1)
    qr_b = qr.reshape(b, nb, ATTN_Q_BLOCK, h, QK_ROPE).swapaxes(0, 1)
    out = lax.map(block, (qn_b, qr_b))
    return out.swapaxes(0, 1).reshape(b, lq, h * V_DIM)


def ssd_chunked(x, dt, a_neg, bm, cm, h0, need_y):
    b, l, nh, p = x.shape
    g, n = bm.shape[2], bm.shape[3]
    e = nh // g
    nc = l // SSD_CHUNK
    xg = (x * dt[..., None]).reshape(b, nc, SSD_CHUNK, g, e, p)
    acum = jnp.cumsum((dt * a_neg).reshape(b, nc, SSD_CHUNK, g, e), axis=2)
    bc = bm.reshape(b, nc, SSD_CHUNK, g, n)
    cc = cm.reshape(b, nc, SSD_CHUNK, g, n)
    decay_to_end = jnp.exp(acum[:, :, -1:] - acum)
    states = jnp.einsum('bcsgn,bcsge,bcsgep->bcgepn', bc, decay_to_end, xg)
    chunk_decay = jnp.exp(acum[:, :, -1])

    def step(hc, inp):
        s, dcy = inp
        return dcy[..., None, None] * hc + s, hc

    h_final, h_prev = lax.scan(step, h0, (jnp.moveaxis(states, 1, 0), jnp.moveaxis(chunk_decay, 1, 0)))
    if not need_y:
        return None, h_final
    h_prev = jnp.moveaxis(h_prev, 0, 1)
    seg = acum[:, :, :, None] - acum[:, :, None, :]
    lower = jnp.tril(jnp.ones((SSD_CHUNK, SSD_CHUNK), bool))[:, :, None, None]
    lmat = jnp.exp(jnp.where(lower, seg, -jnp.inf))
    cb = jnp.einsum('bcqgn,bcsgn->bcqsg', cc, bc)
    y_diag = jnp.einsum('bcqsg,bcqsge,bcsgep->bcqgep', cb, lmat, xg)
    y_off = jnp.einsum('bcqgn,bcgepn,bcqge->bcqgep', cc, h_prev, jnp.exp(acum))
    return (y_diag + y_off).reshape(b, l, nh, p), h_final


def ssd_mixer(z, xbc, dt_raw, conv_w, conv_b, a_log, dt_bias, d_skip, norm_w, init_states, need_y):
    b, l, _ = xbc.shape
    xbc = jax.nn.silu(dwconv_centred(xbc, conv_w, conv_b))
    xs = xbc[..., :SSD_INNER].reshape(b, l, SSD_HEADS, SSD_HEADDIM)
    bm = xbc[..., SSD_INNER:SSD_INNER + SSD_GROUPS * SSD_STATE].reshape(b, l, SSD_GROUPS, SSD_STATE)
    cm = xbc[..., SSD_INNER + SSD_GROUPS * SSD_STATE:].reshape(b, l, SSD_GROUPS, SSD_STATE)
    ys, finals = [], []
    for d in range(2):
        flip = (lambda t: jnp.flip(t, axis=1)) if d == 1 else (lambda t: t)
        dt = jax.nn.softplus(dt_raw[..., d * SSD_HEADS:(d + 1) * SSD_HEADS] + dt_bias[d])
        a_neg = -jnp.exp(a_log[d])
        if init_states is None:
            h0 = jnp.zeros((b, SSD_GROUPS, SSD_HEADS // SSD_GROUPS, SSD_HEADDIM, SSD_STATE), xs.dtype)
        else:
            h0 = init_states[d]
        y, h_final = ssd_chunked(flip(xs), flip(dt), a_neg, flip(bm), flip(cm), h0, need_y)
        finals.append(h_final)
        if need_y:
            ys.append(flip(y))
    if not need_y:
        return None, finals
    y = ys[0] + ys[1] + xs * d_skip[:, None]
    y = (y.reshape(b, l, SSD_INNER) * jax.nn.silu(z)).reshape(b, l, SSD_GROUPS, SSD_INNER // SSD_GROUPS)
    y = rms_norm(y, norm_w.reshape(SSD_GROUPS, SSD_INNER // SSD_GROUPS))
    return y.reshape(b, l, SSD_INNER), finals


def _lin_combine(left, right):
    a1, b1 = left
    a2, b2 = right
    return a1 * a2, a2 * b1 + b2


def rglru_mixer(xr, gy, conv_w, conv_b, wa, ba, wx, bx, lam, init_states, need_y):
    b, l, _ = xr.shape
    xr = dwconv_centred(xr, conv_w, conv_b)
    hs, finals = [], []
    for d in range(2):
        xd = jnp.flip(xr, axis=1) if d == 1 else xr
        xblk = xd.reshape(b, l, LRU_BLOCKS, LRU_BW)
        r = jax.nn.sigmoid(jnp.einsum('blkc,kcd->blkd', xblk, wa[d]).reshape(b, l, LRU_WIDTH) + ba[d])
        i = jax.nn.sigmoid(jnp.einsum('blkc,kcd->blkd', xblk, wx[d]).reshape(b, l, LRU_WIDTH) + bx[d])
        log_a = -LRU_C * r * jax.nn.softplus(-lam[d])
        a = jnp.exp(log_a)
        bt = jnp.sqrt(-jnp.expm1(2 * log_a)) * (i * xd)
        a_cum, h = lax.associative_scan(_lin_combine, (a, bt), axis=1)
        if init_states is not None:
            h = h + a_cum * init_states[d][:, None, :]
        finals.append(h[:, -1])
        if need_y:
            hs.append(jnp.flip(h, axis=1) if d == 1 else h)
    if not need_y:
        return None, finals
    return (hs[0] + hs[1]) * jax.nn.gelu(gy), finals


def merge_branches(branches, gate_logits, w_branch, w_out):
    b, l, _ = gate_logits.shape
    stacked = jnp.stack(branches, axis=2)
    proj = jnp.einsum('blkw,kwd->blkd', stacked, w_branch)
    gates = jax.nn.sigmoid(gate_logits.reshape(b, l, N_BRANCH, D_MODEL))
    return jnp.sum(gates * proj, axis=2) @ w_out


def expert_dispatch(t, eid, ewt, w1, w3, w2):
    n, d = t.shape
    m = n * TOP_K
    flat_e = eid.reshape(m)
    order = jnp.argsort(flat_e)
    e_sorted = flat_e[order]
    tok_sorted = (order // TOP_K).astype(jnp.int32)
    wt_sorted = ewt.reshape(m)[order]
    counts = jnp.zeros((N_EXPERTS,), jnp.int32).at[flat_e].add(1)
    padded = (counts + EXPERT_BLOCK - 1) // EXPERT_BLOCK * EXPERT_BLOCK
    start = jnp.cumsum(counts) - counts
    pend = jnp.cumsum(padded)
    pstart = pend - padded
    dest = pstart[e_sorted] + jnp.arange(m, dtype=jnp.int32) - start[e_sorted]
    n_blocks = -(-m // EXPERT_BLOCK) + N_EXPERTS
    row_tok = jnp.full((n_blocks * EXPERT_BLOCK,), n, jnp.int32).at[dest].set(tok_sorted)
    t_pad = jnp.concatenate([t, jnp.zeros((1, d), t.dtype)], axis=0)
    xin = t_pad[row_tok].reshape(n_blocks, EXPERT_BLOCK, d)
    blk_e = jnp.minimum(jnp.searchsorted(pend, jnp.arange(n_blocks, dtype=jnp.int32) * EXPERT_BLOCK, side='right'), N_EXPERTS - 1)

    def run(args):
        xb, e = args
        hid = jax.nn.silu(xb @ w1[e]) * (xb @ w3[e])
        return hid @ w2[e]

    yout = lax.map(run, (xin, blk_e)).reshape(n_blocks * EXPERT_BLOCK, d)
    contrib = yout[dest] * wt_sorted[:, None].astype(yout.dtype)
    return jax.ops.segment_sum(contrib, tok_sorted, num_segments=n)


def hier_moe(t, wg, bg, we, be, w1, w3, w2):
    n = t.shape[0]
    glog = (t @ wg + bg).astype(jnp.float32)
    gprob = jax.nn.softmax(glog, axis=-1)
    _, gsel = lax.top_k(glog, 1)
    gval = jnp.take_along_axis(gprob, gsel, axis=-1)
    elog = (t @ we + be).astype(jnp.float32).reshape(n, N_GROUPS, EXPERTS_PER_GROUP)
    elog_g = jnp.take_along_axis(elog, gsel[:, :, None], axis=1)[:, 0]
    top_v, top_i = lax.top_k(elog_g, TOP_K)
    ewt = jax.nn.softmax(top_v, axis=-1) * gval
    eid = gsel * EXPERTS_PER_GROUP + top_i
    return expert_dispatch(t, eid, ewt, w1, w3, w2)


def setup_inputs(seed: int = 0) -> dict:
    key = jax.random.key(seed)
    ks = iter(jax.random.split(key, 48))
    f32 = jnp.float32

    def nrm(shape, scale):
        return jax.random.normal(next(ks), shape, f32) * scale

    def gain(shape):
        return 1.0 + nrm(shape, 0.01)

    L = DEPTH
    inp = {}
    inp['x'] = nrm((BATCH, SEQ, D_MODEL), 1.0)
    inp['c'] = nrm((BATCH, D_MODEL), 1.0)
    inp['ctx'] = nrm((BATCH, CTX_LEN, D_MODEL), 1.0)
    inp['c_ctx'] = nrm((D_MODEL,), 1.0)
    inp['w_mod'] = nrm((L, D_MODEL, 6 * D_MODEL), 0.5 * D_MODEL ** -0.5)
    inp['b_mod'] = nrm((L, 6 * D_MODEL), 0.01)
    inp['w_in'] = nrm((L, D_MODEL, IN_COLS), D_MODEL ** -0.5)
    inp['q_norm_w'] = gain((L, Q_LORA))
    inp['kv_norm_w'] = gain((L, KV_LORA))
    inp['w_uq'] = nrm((L, Q_LORA, MLA_HEADS * (QK_NOPE + QK_ROPE)), Q_LORA ** -0.5)
    inp['w_ukv'] = nrm((L, KV_LORA, MLA_HEADS * (QK_NOPE + V_DIM)), KV_LORA ** -0.5)
    inp['ssd_conv_w'] = nrm((L, CONV_W, SSD_CONV_CH), CONV_W ** -0.5)
    inp['ssd_conv_b'] = nrm((L, SSD_CONV_CH), 0.01)
    inp['ssd_a_log'] = jnp.log(jax.random.uniform(next(ks), (L, 2, SSD_HEADS), f32, 1.0, 16.0))
    dt0 = jnp.exp(jax.random.uniform(next(ks), (L, 2, SSD_HEADS), f32, math.log(1e-3), math.log(1e-1)))
    inp['ssd_dt_bias'] = dt0 + jnp.log(-jnp.expm1(-dt0))
    inp['ssd_d'] = gain((L, SSD_HEADS))
    inp['ssd_norm_w'] = gain((L, SSD_INNER))
    inp['lru_conv_w'] = nrm((L, CONV_W, LRU_WIDTH), CONV_W ** -0.5)
    inp['lru_conv_b'] = nrm((L, LRU_WIDTH), 0.01)
    inp['lru_wa'] = nrm((L, 2, LRU_BLOCKS, LRU_BW, LRU_BW), LRU_BW ** -0.5)
    inp['lru_ba'] = nrm((L, 2, LRU_WIDTH), 0.01)
    inp['lru_wx'] = nrm((L, 2, LRU_BLOCKS, LRU_BW, LRU_BW), LRU_BW ** -0.5)
    inp['lru_bx'] = nrm((L, 2, LRU_WIDTH), 0.01)
    a0 = jax.random.uniform(next(ks), (L, 2, LRU_WIDTH), f32, 0.9, 0.999)
    s0 = a0 ** (1.0 / LRU_C)
    inp['lru_lambda'] = jnp.log(s0) - jnp.log1p(-s0)
    inp['w_branch'] = nrm((L, N_BRANCH, BRANCH_W, D_MODEL), BRANCH_W ** -0.5)
    inp['w_out'] = nrm((L, D_MODEL, D_MODEL), BETA * D_MODEL ** -0.5)
    inp['ln1_g'] = gain((L, D_MODEL))
    inp['ln1_b'] = nrm((L, D_MODEL), 0.01)
    inp['ln2_g'] = gain((L, D_MODEL))
    inp['ln2_b'] = nrm((L, D_MODEL), 0.01)
    inp['router_wg'] = nrm((L, D_MODEL, N_GROUPS), D_MODEL ** -0.5)
    inp['router_bg'] = nrm((L, N_GROUPS), 0.01)
    inp['router_we'] = nrm((L, D_MODEL, N_EXPERTS), D_MODEL ** -0.5)
    inp['router_be'] = nrm((L, N_EXPERTS), 0.01)
    inp['exp_w1'] = nrm((L, N_EXPERTS, D_MODEL, EXPERT_HIDDEN), D_MODEL ** -0.5)
    inp['exp_w3'] = nrm((L, N_EXPERTS, D_MODEL, EXPERT_HIDDEN), D_MODEL ** -0.5)
    inp['exp_w2'] = nrm((L, N_EXPERTS, EXPERT_HIDDEN, D_MODEL), BETA * EXPERT_HIDDEN ** -0.5)
    return inp


def reference(x, c, ctx, c_ctx, w_mod, b_mod, w_in, q_norm_w, kv_norm_w, w_uq, w_ukv,
              ssd_conv_w, ssd_conv_b, ssd_a_log, ssd_dt_bias, ssd_d, ssd_norm_w,
              lru_conv_w, lru_conv_b, lru_wa, lru_ba, lru_wx, lru_bx, lru_lambda,
              w_branch, w_out, ln1_g, ln1_b, ln2_g, ln2_b,
              router_wg, router_bg, router_we, router_be, exp_w1, exp_w3, exp_w2):
    bsz, seq, _ = x.shape
    n_ctx = ctx.shape[1]
    rows = seq // GRID_W
    row_pos = jnp.repeat(jnp.arange(rows, dtype=jnp.float32), GRID_W)
    col_pos = (jnp.arange(rows * GRID_W) % GRID_W).astype(jnp.float32)
    inv_freq = ROPE_THETA ** (-jnp.arange(ROPE_FREQS, dtype=jnp.float32) / ROPE_FREQS)
    ang = jnp.stack([row_pos[:, None] * inv_freq, col_pos[:, None] * inv_freq], axis=1)
    rope = (jnp.cos(ang).astype(x.dtype), jnp.sin(ang).astype(x.dtype))
    silu_c = jax.nn.silu(c)
    silu_cc = jax.nn.silu(c_ctx)
    h, hc = x, ctx
    for layer in range(DEPTH):
        last = layer == DEPTH - 1
        sh1, sc1, g1, sh2, sc2, g2 = [m[:, None, :] for m in jnp.split(silu_c @ w_mod[layer] + b_mod[layer], 6, axis=-1)]
        sh1c, sc1c, g1c, sh2c, sc2c, g2c = jnp.split(silu_cc @ w_mod[layer] + b_mod[layer], 6, axis=-1)
        cq_l, ckv_l, kr_l, z_l, xbc_l, dt_l, lx_l, lg_l, gate_l = split_in(modulate(h, sh1, sc1) @ w_in[layer])
        cq_c, ckv_c, kr_c, z_c, xbc_c, dt_c, lx_c, lg_c, gate_c = split_in(modulate(hc, sh1c, sc1c) @ w_in[layer])
        kn_c, v_c, kr_c = mla_kv(ckv_c, kr_c, kv_norm_w[layer], w_ukv[layer], None)
        kn_l, v_l, kr_l = mla_kv(ckv_l, kr_l, kv_norm_w[layer], w_ukv[layer], rope)
        qn_l, qr_l = mla_q(cq_l, q_norm_w[layer], w_uq[layer], rope)
        att_l = mla_attend(qn_l, qr_l, jnp.concatenate([kn_c, kn_l], axis=1),
                           jnp.concatenate([kr_c, kr_l], axis=1), jnp.concatenate([v_c, v_l], axis=1))
        ssd_p = (ssd_conv_w[layer], ssd_conv_b[layer], ssd_a_log[layer], ssd_dt_bias[layer], ssd_d[layer], ssd_norm_w[layer])
        y_ssd_c, st_ssd = ssd_mixer(z_c, xbc_c, dt_c, *ssd_p, None, not last)
        y_ssd_l, _ = ssd_mixer(z_l, xbc_l, dt_l, *ssd_p, st_ssd, True)
        lru_p = (lru_conv_w[layer], lru_conv_b[layer], lru_wa[layer], lru_ba[layer], lru_wx[layer], lru_bx[layer], lru_lambda[layer])
        y_lru_c, st_lru = rglru_mixer(lx_c, lg_c, *lru_p, None, not last)
        y_lru_l, _ = rglru_mixer(lx_l, lg_l, *lru_p, st_lru, True)
        mix_l = merge_branches((att_l, y_ssd_l, y_lru_l), gate_l, w_branch[layer], w_out[layer])
        h = layer_norm(ALPHA * h + g1 * mix_l, ln1_g[layer], ln1_b[layer])
        if not last:
            qn_c, qr_c = mla_q(cq_c, q_norm_w[layer], w_uq[layer], None)
            att_c = mla_attend(qn_c, qr_c, kn_c, kr_c, v_c)
            mix_c = merge_branches((att_c, y_ssd_c, y_lru_c), gate_c, w_branch[layer], w_out[layer])
            hc = layer_norm(ALPHA * hc + g1c * mix_c, ln1_g[layer], ln1_b[layer])
        moe_p = (router_wg[layer], router_bg[layer], router_we[layer], router_be[layer], exp_w1[layer], exp_w3[layer], exp_w2[layer])
        u2 = modulate(h, sh2, sc2).reshape(bsz * seq, D_MODEL)
        if last:
            f_l = hier_moe(u2, *moe_p).reshape(bsz, seq, D_MODEL)
        else:
            u2c = modulate(hc, sh2c, sc2c).reshape(bsz * n_ctx, D_MODEL)
            f = hier_moe(jnp.concatenate([u2c, u2], axis=0), *moe_p)
            f_c = f[:bsz * n_ctx].reshape(bsz, n_ctx, D_MODEL)
            f_l = f[bsz * n_ctx:].reshape(bsz, seq, D_MODEL)
            hc = layer_norm(ALPHA * hc + g2c * f_c, ln2_g[layer], ln2_b[layer])
        h = layer_norm(ALPHA * h + g2 * f_l, ln2_g[layer], ln2_b[layer])
    return h
```

```python
import functools
import math

import jax
import jax.numpy as jnp
from jax import lax
from jax.experimental import pallas as pl
from jax.experimental.pallas import tpu as pltpu

F32 = jnp.float32
BF16 = jnp.bfloat16

D_MODEL = 1024
GRID_W = 64
N_BRANCH = 3

MLA_HEADS = 8
QK_NOPE = 128
QK_ROPE = 64
V_DIM = 128
Q_LORA = 512
KV_LORA = 256
ROPE_FREQS = QK_ROPE // 4
ROPE_THETA = 10000.0
ATTN_SCALE = (QK_NOPE + QK_ROPE) ** -0.5
QK_PAD = 256

SSD_HEADDIM = 64
SSD_INNER = 1024
SSD_HEADS = 16
SSD_GROUPS = 4
SSD_STATE = 128
SSD_CHUNK = 128
GROUP_W = SSD_INNER // SSD_GROUPS

CONV_W = 4
LRU_WIDTH = 1024
LRU_BW = 64
LRU_C = 8.0
LRU_TILE = 128
LRU_GROUP = 256

N_GROUPS = 4
EXPERTS_PER_GROUP = 8
N_EXPERTS = 32
TOP_K = 2
EXPERT_HIDDEN = 512
EXPERT_BLOCK = 128

LN_EPS = 1e-5
RMS_EPS = 1e-6

LANES = 128
SUBLANES = 8
ROW_TILE = 256
VMEM_LIMIT = 56 * 1024 * 1024

COL_Z = 0
COL_LX = 1024
COL_LG = 2048
COL_GATE = 3072
COL_XBC = 6144
COL_CQ = 8192
COL_CKV = 8704
COL_KR = 8960
COL_DT = 9088
DT_DIR_STRIDE = 64
IN_COLS_PAD = 9216


def _cparams(**kw):
    return pltpu.CompilerParams(vmem_limit_bytes=VMEM_LIMIT, **kw)


def _split3(x):
    hi = x.astype(BF16)
    r = x - hi.astype(F32)
    mid = r.astype(BF16)
    lo = (r - mid.astype(F32)).astype(BF16)
    return hi, mid, lo


def _dot(a, b):
    return jnp.dot(a, b, preferred_element_type=F32)


def _dot_nt(a, b):
    return lax.dot_general(a, b, (((1,), (1,)), ((), ())), preferred_element_type=F32)


def _sigmoid(x):
    return 1.0 / (1.0 + jnp.exp(-x))


def _silu(x):
    return x * _sigmoid(x)


def _softplus(x):
    return jnp.maximum(x, 0.0) + jnp.log(1.0 + jnp.exp(-jnp.abs(x)))


def _layer_norm(t):
    mu = jnp.mean(t, axis=-1, keepdims=True)
    c = t - mu
    var = jnp.mean(c * c, axis=-1, keepdims=True)
    return c * lax.rsqrt(var + LN_EPS)


def _mod_kernel(c_ref, w_ref, b_ref, o_ref):
    s = _silu(c_ref[...]).astype(BF16)
    o_ref[...] = _dot(s, w_ref[...].astype(BF16)) + b_ref[...]


def _mod_vectors(cvec, w_mod, b_mod):
    depth, d, n = w_mod.shape
    tn = 1536
    return pl.pallas_call(
        _mod_kernel,
        grid=(depth, n // tn),
        in_specs=[pl.BlockSpec((16, d), lambda l, j: (0, 0)),
                  pl.BlockSpec((None, d, tn), lambda l, j: (l, 0, j)),
                  pl.BlockSpec((None, 1, tn), lambda l, j: (l, 0, j))],
        out_specs=pl.BlockSpec((None, 16, tn), lambda l, j: (l, 0, j)),
        out_shape=jax.ShapeDtypeStruct((depth, 16, n), F32),
        compiler_params=_cparams(),
        name="mod_vectors",
    )(cvec, w_mod, b_mod.reshape(depth, 1, n))


def _lnmod_kernel(h_ref, sh_ref, sc_ref, o_ref):
    o_ref[...] = (_layer_norm(h_ref[...]) * (1.0 + sc_ref[...]) + sh_ref[...]).astype(o_ref.dtype)


def _mod_spec(tiles_per_sample, k):
    def imap(i):
        return (jnp.where(i % tiles_per_sample == 0, 8, i // tiles_per_sample), 0, k)
    return pl.BlockSpec((None, 1, D_MODEL), imap)


def _ln_modulate(h, mods, tps):
    n = h.shape[0]
    return pl.pallas_call(
        _lnmod_kernel,
        grid=(n // ROW_TILE,),
        in_specs=[pl.BlockSpec((ROW_TILE, D_MODEL), lambda i: (i, 0)),
                  _mod_spec(tps, 0), _mod_spec(tps, 1)],
        out_specs=pl.BlockSpec((ROW_TILE, D_MODEL), lambda i: (i, 0)),
        out_shape=jax.ShapeDtypeStruct((n, D_MODEL), BF16),
        compiler_params=_cparams(),
        name="ln_modulate",
    )(h, mods, mods)


def _matmul_kernel(a_ref, w_ref, o_ref):
    o_ref[...] = _dot(a_ref[...], w_ref[...])


def _matmul(a, w, tm, tn):
    m, k = a.shape
    n = w.shape[1]
    return pl.pallas_call(
        _matmul_kernel,
        grid=(m // tm, n // tn),
        in_specs=[pl.BlockSpec((tm, k), lambda i, j: (i, 0)),
                  pl.BlockSpec((k, tn), lambda i, j: (0, j))],
        out_specs=pl.BlockSpec((tm, tn), lambda i, j: (i, j)),
        out_shape=jax.ShapeDtypeStruct((m, n), F32),
        compiler_params=_cparams(),
        name="in_proj",
    )(a, w)


def _rope128(t, cos, sin):
    lane = lax.broadcasted_iota(jnp.int32, t.shape, 1)
    partner = jnp.where(lane % (2 * ROPE_FREQS) < ROPE_FREQS,
                        pltpu.roll(t, LANES - ROPE_FREQS, 1), pltpu.roll(t, ROPE_FREQS, 1))
    return t * cos + partner * sin


def _rms(t, gain):
    return t * lax.rsqrt(jnp.mean(t * t, axis=-1, keepdims=True) + RMS_EPS) * gain


def _mla_prep_kernel(cq_ref, ckv_ref, kr_ref, qg_ref, kvg_ref, wq_ref, wkv_ref, cos_ref, sin_ref,
                     q_ref, k_ref, v_ref):
    cos = cos_ref[...]
    sin = sin_ref[...]
    q = _dot(_rms(cq_ref[...], qg_ref[...]).astype(BF16), wq_ref[...])
    kv = _dot(_rms(ckv_ref[...], kvg_ref[...]).astype(BF16), wkv_ref[...])
    krz = _rope128(kr_ref[...], cos, sin).astype(BF16)
    for h in range(MLA_HEADS):
        c0 = h * QK_PAD
        q_ref[:, c0:c0 + QK_NOPE] = q[:, c0:c0 + QK_NOPE].astype(BF16)
        q_ref[:, c0 + QK_NOPE:c0 + QK_PAD] = _rope128(q[:, c0 + QK_NOPE:c0 + QK_PAD], cos, sin).astype(BF16)
        k_ref[:, c0:c0 + QK_NOPE] = kv[:, h * QK_NOPE:(h + 1) * QK_NOPE].astype(BF16)
        k_ref[:, c0 + QK_NOPE:c0 + QK_PAD] = krz
    v_ref[...] = kv[:, MLA_HEADS * QK_NOPE:].astype(BF16)


def _mla_prep(proj, q_gain, kv_gain, wq, wkv, cos_t, sin_t, tps):
    n = proj.shape[0]
    t = ROW_TILE
    row = lambda w, cb: pl.BlockSpec((t, w), lambda i: (i, cb))
    const = lambda shape: pl.BlockSpec(shape, lambda i: (0, 0))
    return pl.pallas_call(
        _mla_prep_kernel,
        grid=(n // t,),
        in_specs=[row(Q_LORA, COL_CQ // Q_LORA), row(KV_LORA, COL_CKV // KV_LORA), row(LANES, COL_KR // LANES),
                  const((1, Q_LORA)), const((1, KV_LORA)),
                  const((Q_LORA, MLA_HEADS * QK_PAD)), const((KV_LORA, 2 * MLA_HEADS * QK_NOPE)),
                  pl.BlockSpec((t, LANES), lambda i: (i % tps, 0)),
                  pl.BlockSpec((t, LANES), lambda i: (i % tps, 0))],
        out_specs=[pl.BlockSpec((t, MLA_HEADS * QK_PAD), lambda i: (i, 0)),
                   pl.BlockSpec((t, MLA_HEADS * QK_PAD), lambda i: (i, 0)),
                   pl.BlockSpec((t, MLA_HEADS * V_DIM), lambda i: (i, 0))],
        out_shape=[jax.ShapeDtypeStruct((n, MLA_HEADS * QK_PAD), BF16),
                   jax.ShapeDtypeStruct((n, MLA_HEADS * QK_PAD), BF16),
                   jax.ShapeDtypeStruct((n, MLA_HEADS * V_DIM), BF16)],
        compiler_params=_cparams(),
        name="mla_prep",
    )(proj, proj, proj, q_gain, kv_gain, wq, wkv, cos_t, sin_t)


def _attn_kernel(q_ref, k_ref, v_ref, o_ref, *, n_ctx):
    def attend(nk):
        s = _dot_nt(q_ref[...], k_ref[0:nk, :])
        m = jnp.max(s, axis=-1, keepdims=True)
        p = jnp.exp((s - m) * ATTN_SCALE)
        l = jnp.sum(p, axis=-1, keepdims=True)
        o = _dot(p.astype(BF16), v_ref[0:nk, :])
        o_ref[...] = (o / l).astype(o_ref.dtype)

    @pl.when(pl.program_id(2) == 0)
    def _():
        attend(n_ctx)

    @pl.when(pl.program_id(2) != 0)
    def _():
        attend(k_ref.shape[0])


def _attention(q, k, v, n_ctx):
    b, s, _ = q.shape
    tq = ROW_TILE
    return pl.pallas_call(
        functools.partial(_attn_kernel, n_ctx=n_ctx),
        grid=(b, MLA_HEADS, s // tq),
        in_specs=[pl.BlockSpec((None, tq, QK_PAD), lambda bi, h, i: (bi, i, h)),
                  pl.BlockSpec((None, s, QK_PAD), lambda bi, h, i: (bi, 0, h)),
                  pl.BlockSpec((None, s, V_DIM), lambda bi, h, i: (bi, 0, h))],
        out_specs=pl.BlockSpec((None, tq, V_DIM), lambda bi, h, i: (bi, i, h)),
        out_shape=jax.ShapeDtypeStruct((b, s, MLA_HEADS * V_DIM), BF16),
        compiler_params=_cparams(),
        name="attention",
    )(q, k, v)


def _conv_kernel(x_ref, p_ref, n_ref, w_ref, b_ref, o_ref, *, silu, tps):
    j = pl.program_id(1)
    has_prev = (j > 1).astype(F32)
    has_next = jnp.logical_and(j > 0, j < tps - 1).astype(F32)
    xe = jnp.concatenate([p_ref[...] * has_prev, x_ref[...], n_ref[...] * has_next], axis=0)
    rows = xe.shape[0]
    t = x_ref.shape[0]
    w = w_ref[...]
    y = b_ref[...] + w[2:3, :] * x_ref[...]
    y = y + w[0:1, :] * pltpu.roll(xe, 2, 0)[SUBLANES:SUBLANES + t]
    y = y + w[1:2, :] * pltpu.roll(xe, 1, 0)[SUBLANES:SUBLANES + t]
    y = y + w[3:4, :] * pltpu.roll(xe, rows - 1, 0)[SUBLANES:SUBLANES + t]
    o_ref[...] = _silu(y) if silu else y


def _dwconv(proj3, col0, width, w, bias, silu):
    b, s, _ = proj3.shape
    t = ROW_TILE
    tc = 1024
    cb = col0 // tc
    tps = s // t
    hb = t // SUBLANES
    last = s // SUBLANES - 1
    return pl.pallas_call(
        functools.partial(_conv_kernel, silu=silu, tps=tps),
        grid=(b, tps, width // tc),
        in_specs=[pl.BlockSpec((None, t, tc), lambda bi, j, c: (bi, j, cb + c)),
                  pl.BlockSpec((None, SUBLANES, tc), lambda bi, j, c: (bi, jnp.maximum(j * hb - 1, 0), cb + c)),
                  pl.BlockSpec((None, SUBLANES, tc), lambda bi, j, c: (bi, jnp.minimum((j + 1) * hb, last), cb + c)),
                  pl.BlockSpec((CONV_W, tc), lambda bi, j, c: (0, c)),
                  pl.BlockSpec((1, tc), lambda bi, j, c: (0, c))],
        out_specs=pl.BlockSpec((None, t, tc), lambda bi, j, c: (bi, j, c)),
        out_shape=jax.ShapeDtypeStruct((b, s, width), F32),
        compiler_params=_cparams(),
        name="dwconv",
    )(proj3, proj3, proj3, w, bias)


def _ssd_kernel(xs_ref, b_ref, c_ref, dt_ref, dtb_ref, aneg_ref, e_ref, y_ref, h_ref, *, rev):
    @pl.when(pl.program_id(1) == 0)
    def _():
        h_ref[...] = jnp.zeros_like(h_ref)

    q = SSD_CHUNK
    dt = _softplus(dt_ref[...] + dtb_ref[...])
    a = dt * aneg_ref[...]
    row = lax.broadcasted_iota(jnp.int32, (q, q), 0)
    col = lax.broadcasted_iota(jnp.int32, (q, q), 1)
    tri = (col >= row) if rev else (col <= row)
    tri_b = jnp.where(tri, 1.0, 0.0).astype(BF16)
    a3 = _split3(a)
    acum = _dot(tri_b, a3[0]) + _dot(tri_b, a3[1]) + _dot(tri_b, a3[2])
    total = jnp.sum(a, axis=0, keepdims=True)
    stack = jnp.concatenate([dt, total - acum, acum, jnp.broadcast_to(total, (SUBLANES, LANES))], axis=0)
    e = e_ref[...]
    s3 = _split3(stack)
    ex = _dot(s3[0], e) + _dot(s3[1], e) + _dot(s3[2], e)
    dt_e = ex[0:q]
    to_end_e = jnp.exp(ex[q:2 * q])
    from_start_e = jnp.exp(ex[2 * q:3 * q])
    chunk_decay_e = jnp.exp(ex[3 * q:3 * q + 1])
    xg = xs_ref[...] * dt_e
    xg_b = xg.astype(BF16)
    w_b = (xg * to_end_e).astype(BF16)
    acum_t = acum.T
    base = DT_DIR_STRIDE if rev else 0
    for g in range(SSD_GROUPS):
        gs = slice(g * GROUP_W, (g + 1) * GROUP_W)
        bg = b_ref[:, g * SSD_STATE:(g + 1) * SSD_STATE]
        cg = c_ref[:, g * SSD_STATE:(g + 1) * SSD_STATE].astype(BF16)
        cb = _dot_nt(cg, bg.astype(BF16))
        h_prev = h_ref[:, gs]
        y_off = _dot(cg, h_prev.astype(BF16)) * from_start_e[:, gs]
        h_ref[:, gs] = chunk_decay_e[:, gs] * h_prev + _dot(bg.T.astype(BF16), w_b[:, gs])
        parts = []
        for hh in range(SSD_HEADS // SSD_GROUPS):
            head = g * (SSD_HEADS // SSD_GROUPS) + hh
            c = base + head
            seg = acum[:, c:c + 1] - acum_t[c:c + 1, :]
            lmat = jnp.exp(jnp.where(tri, seg, -1e30))
            parts.append(_dot((cb * lmat).astype(BF16), xg_b[:, head * SSD_HEADDIM:(head + 1) * SSD_HEADDIM]))
        y_ref[:, gs] = jnp.concatenate(parts, axis=1) + y_off


def _ssd_order(step, n_ctx_chunks, n_chunks, rev):
    if not rev:
        return step
    return jnp.where(step < n_ctx_chunks, n_ctx_chunks - 1 - step, n_chunks - 1 + n_ctx_chunks - step)


def _ssd_scan(xbc_conv, proj3, dt_bias_row, a_neg_row, expand, n_ctx, rev):
    b, s, _ = xbc_conv.shape
    q = SSD_CHUNK
    nchunks = s // q
    order = functools.partial(_ssd_order, n_ctx_chunks=n_ctx // q, n_chunks=nchunks, rev=rev)
    const = lambda shape: pl.BlockSpec(shape, lambda bi, i: (0, 0))
    return pl.pallas_call(
        functools.partial(_ssd_kernel, rev=rev),
        grid=(b, nchunks),
        in_specs=[pl.BlockSpec((None, q, SSD_INNER), lambda bi, i: (bi, order(i), 0)),
                  pl.BlockSpec((None, q, SSD_GROUPS * SSD_STATE), lambda bi, i: (bi, order(i), 2)),
                  pl.BlockSpec((None, q, SSD_GROUPS * SSD_STATE), lambda bi, i: (bi, order(i), 3)),
                  pl.BlockSpec((None, q, LANES), lambda bi, i: (bi, order(i), COL_DT // LANES)),
                  const((1, LANES)), const((1, LANES)), const((LANES, SSD_INNER))],
        out_specs=pl.BlockSpec((None, q, SSD_INNER), lambda bi, i: (bi, order(i), 0)),
        out_shape=jax.ShapeDtypeStruct((b, s, SSD_INNER), F32),
        scratch_shapes=[pltpu.VMEM((SSD_STATE, SSD_INNER), F32)],
        compiler_params=_cparams(),
        name="ssd_rev" if rev else "ssd_fwd",
    )(xbc_conv, xbc_conv, xbc_conv, proj3, dt_bias_row, a_neg_row, expand)


def _lru_kernel(x_ref, w_ref, ba_ref, bx_ref, lam_ref, o_ref, a_s, b_s, h_s, carry, *, rev):
    nb, t, _ = x_ref.shape
    per = LRU_GROUP // LANES

    @pl.when(pl.program_id(0) == 0)
    def _():
        carry[...] = jnp.zeros_like(carry)

    decay = -LRU_C * _softplus(-lam_ref[...])
    for bi in range(nb):
        for g in range(LRU_WIDTH // LRU_GROUP):
            gs = slice(g * LRU_GROUP, (g + 1) * LRU_GROUP)
            xd = x_ref[bi, :, gs]
            ri = _dot(xd.astype(BF16), w_ref[g])
            r = _sigmoid(ri[:, :LRU_GROUP] + ba_ref[:, gs])
            i = _sigmoid(ri[:, LRU_GROUP:] + bx_ref[:, gs])
            log_a = decay[:, gs] * r
            a = jnp.exp(log_a)
            bt = jnp.sqrt(jnp.tanh(-log_a) * (1.0 + a * a)) * (i * xd)
            for cc in range(per):
                ls = slice(cc * LANES, (cc + 1) * LANES)
                a_s[g * per + cc, pl.ds(bi, t, stride=nb), :] = a[:, ls]
                b_s[g * per + cc, pl.ds(bi, t, stride=nb), :] = bt[:, ls]

    def step(k, h):
        tt = (t - 1 - k) if rev else k
        r0 = pl.multiple_of(tt * nb, nb)
        h = a_s[:, pl.ds(r0, nb), :] * h + b_s[:, pl.ds(r0, nb), :]
        h_s[:, pl.ds(r0, nb), :] = h
        return h

    carry[...] = lax.fori_loop(0, t, step, carry[...])
    for bi in range(nb):
        for cc in range(LRU_WIDTH // LANES):
            o_ref[bi, :, cc * LANES:(cc + 1) * LANES] = h_s[cc, pl.ds(bi, t, stride=nb), :]


def _lru_scan(lx_conv, w_gate, ba, bx, lam, n_ctx, rev):
    b, s, width = lx_conv.shape
    t = LRU_TILE
    ntiles = s // t
    order = functools.partial(_ssd_order, n_ctx_chunks=n_ctx // t, n_chunks=ntiles, rev=rev)
    const = lambda shape: pl.BlockSpec(shape, lambda i: (0,) * len(shape))
    return pl.pallas_call(
        functools.partial(_lru_kernel, rev=rev),
        grid=(ntiles,),
        in_specs=[pl.BlockSpec((b, t, width), lambda i: (0, order(i), 0)),
                  const(w_gate.shape), const((1, width)), const((1, width)), const((1, width))],
        out_specs=pl.BlockSpec((b, t, width), lambda i: (0, order(i), 0)),
        out_shape=jax.ShapeDtypeStruct((b, s, width), F32),
        scratch_shapes=[pltpu.VMEM((width // LANES, t * b, LANES), F32)] * 3
        + [pltpu.VMEM((width // LANES, b, LANES), F32)],
        compiler_params=_cparams(),
        name="lru_rev" if rev else "lru_fwd",
    )(lx_conv, w_gate, ba, bx, lam)


def _gelu_tanh(x):
    return 0.5 * x * (1.0 + jnp.tanh(math.sqrt(2.0 / math.pi) * (x + 0.044715 * (x * x * x))))


def _merge_kernel(att_ref, y0_ref, y1_ref, xs_ref, z_ref, l0_ref, l1_ref, lg_ref, gate_ref, h_ref,
                  dskip_ref, ssdg_ref, wb_ref, wo_ref, g1_ref, lng_ref, lnb_ref, sh2_ref, sc2_ref,
                  h1_ref, u2_ref, *, alpha):
    y = y0_ref[...] + y1_ref[...] + xs_ref[...] * dskip_ref[...]
    y = y * _silu(z_ref[...])
    parts = []
    for g in range(SSD_GROUPS):
        yg = y[:, g * GROUP_W:(g + 1) * GROUP_W]
        parts.append(yg * lax.rsqrt(jnp.mean(yg * yg, axis=-1, keepdims=True) + RMS_EPS))
    y_ssd = jnp.concatenate(parts, axis=1) * ssdg_ref[...]
    y_lru = (l0_ref[...] + l1_ref[...]) * _gelu_tanh(lg_ref[...])
    branches = (att_ref[...], y_ssd.astype(BF16), y_lru.astype(BF16))
    mix = None
    for k in range(N_BRANCH):
        term = _sigmoid(gate_ref[:, k * D_MODEL:(k + 1) * D_MODEL]) * _dot(branches[k], wb_ref[k])
        mix = term if mix is None else mix + term
    out = _dot(mix.astype(BF16), wo_ref[...])
    h1 = _layer_norm(alpha * h_ref[...] + g1_ref[...] * out) * lng_ref[...] + lnb_ref[...]
    h1_ref[...] = h1
    u2 = _layer_norm(h1) * (1.0 + sc2_ref[...]) + sh2_ref[...]
    for s in range(D_MODEL // LANES):
        u2_ref[:, s, :] = u2[:, s * LANES:(s + 1) * LANES]


def _merge(att, y0, y1, xbc_conv, proj, l0, l1, h, d_skip_row, ssd_gain, wb, wo, mods, ln_g, ln_b, tps, alpha):
    n = h.shape[0]
    t = ROW_TILE
    d = D_MODEL
    row = lambda cb: pl.BlockSpec((t, d), lambda i: (i, cb))
    const = lambda shape: pl.BlockSpec(shape, lambda i: (0,) * len(shape))
    return pl.pallas_call(
        functools.partial(_merge_kernel, alpha=alpha),
        grid=(n // t,),
        in_specs=[row(0), row(0), row(0), row(0), row(COL_Z // d), row(0), row(0), row(COL_LG // d),
                  pl.BlockSpec((t, N_BRANCH * d), lambda i: (i, COL_GATE // (N_BRANCH * d))), row(0),
                  const((1, d)), const((1, d)), const(wb.shape), const(wo.shape),
                  _mod_spec(tps, 2), const((1, d)), const((1, d)), _mod_spec(tps, 3), _mod_spec(tps, 4)],
        out_specs=[row(0), pl.BlockSpec((t, d // LANES, LANES), lambda i: (i, 0, 0))],
        out_shape=[jax.ShapeDtypeStruct((n, d), F32),
                   jax.ShapeDtypeStruct((n, d // LANES, LANES), F32)],
        compiler_params=_cparams(),
        name="merge",
    )(att, y0, y1, xbc_conv, proj, l0, l1, proj, proj, h, d_skip_row, ssd_gain, wb, wo,
      mods, ln_g, ln_b, mods, mods)


def _rows_from_tiles(ref):
    return jnp.concatenate([ref[:, s, :] for s in range(ref.shape[1])], axis=1)


def _router_kernel(u_ref, w_ref, b_ref, meta_ref, cnt_ref, count):
    @pl.when(pl.program_id(0) == 0)
    def _():
        count[...] = jnp.zeros_like(count)

    u = _rows_from_tiles(u_ref)
    t = u.shape[0]
    uh = u.astype(BF16)
    ul = (u - uh.astype(F32)).astype(BF16)
    w = w_ref[...]
    wh = w.astype(BF16)
    wl = (w - wh.astype(F32)).astype(BF16)
    logits = _dot(uh, wh) + _dot(uh, wl) + _dot(ul, wh) + b_ref[...]
    lane = lax.broadcasted_iota(jnp.int32, logits.shape, 1)
    neg = -jnp.inf
    big = 4 * LANES

    def top1(vals):
        m = jnp.max(vals, axis=-1, keepdims=True)
        idx = jnp.min(jnp.where(vals == m, lane, big), axis=-1, keepdims=True)
        return m, idx

    glog = jnp.where(lane < N_GROUPS, logits, neg)
    gmax, gsel = top1(glog)
    gval = 1.0 / jnp.sum(jnp.exp(glog - gmax), axis=-1, keepdims=True)
    lo = N_GROUPS + gsel * EXPERTS_PER_GROUP
    elog = jnp.where(jnp.logical_and(lane >= lo, lane < lo + EXPERTS_PER_GROUP), logits, neg)
    v1, i1 = top1(elog)
    v2, i2 = top1(jnp.where(lane == i1, neg, elog))
    e21 = jnp.exp(v2 - v1)
    w1 = gval / (1.0 + e21)
    w2 = gval * e21 / (1.0 + e21)
    e1 = i1 - N_GROUPS
    e2 = i2 - N_GROUPS
    onehot = jnp.logical_or(lane == e1, lane == e2)
    oh = jnp.where(onehot, 1.0, 0.0)
    r = lax.broadcasted_iota(jnp.int32, (t, t), 0)
    c = lax.broadcasted_iota(jnp.int32, (t, t), 1)
    before = jnp.where(c < r, 1.0, 0.0).astype(BF16)
    prefix = _dot(before, oh.astype(BF16)) + count[0:1, :]
    rank1 = jnp.sum(jnp.where(lane == e1, prefix, 0.0), axis=-1, keepdims=True)
    rank2 = jnp.sum(jnp.where(lane == e2, prefix, 0.0), axis=-1, keepdims=True)
    meta = jnp.where(lane == 0, e1.astype(F32), 0.0)
    meta = jnp.where(lane == 1, e2.astype(F32), meta)
    meta = jnp.where(lane == 2, w1, meta)
    meta = jnp.where(lane == 3, w2, meta)
    meta = jnp.where(lane == 4, rank1, meta)
    meta = jnp.where(lane == 5, rank2, meta)
    meta_ref[...] = meta
    count[...] = count[...] + jnp.sum(oh, axis=0, keepdims=True)
    cnt_ref[...] = count[...]


def _router(u2t, w_router, b_router):
    n = u2t.shape[0]
    t = ROW_TILE
    return pl.pallas_call(
        _router_kernel,
        grid=(n // t,),
        in_specs=[pl.BlockSpec((t, D_MODEL // LANES, LANES), lambda i: (i, 0, 0)),
                  pl.BlockSpec((D_MODEL, LANES), lambda i: (0, 0)),
                  pl.BlockSpec((1, LANES), lambda i: (0, 0))],
        out_specs=[pl.BlockSpec((t, LANES), lambda i: (i, 0)),
                   pl.BlockSpec((SUBLANES, LANES), lambda i: (0, 0))],
        out_shape=[jax.ShapeDtypeStruct((n, LANES), F32),
                   jax.ShapeDtypeStruct((SUBLANES, LANES), F32)],
        scratch_shapes=[pltpu.VMEM((SUBLANES, LANES), F32)],
        compiler_params=_cparams(),
        name="router",
    )(u2t, w_router, b_router)


DISPATCH_TILE = 512


def _dispatch_kernel(dest_ref, u_ref, init_ref, x_ref, sem):
    del init_ref
    base = pl.program_id(0) * DISPATCH_TILE

    def copy(k, slot):
        tok = base + k
        return pltpu.make_async_copy(u_ref.at[tok], x_ref.at[dest_ref[TOP_K * tok + slot]], sem)

    def issue(k, carry):
        for slot in range(TOP_K):
            copy(k, slot).start()
        return carry

    def drain(k, carry):
        for slot in range(TOP_K):
            copy(k, slot).wait()
        return carry

    lax.fori_loop(0, DISPATCH_TILE, issue, 0)
    lax.fori_loop(0, DISPATCH_TILE, drain, 0)


def _dispatch(dest_flat, u2t, n_rows):
    n = u2t.shape[0]
    init = jnp.zeros((n_rows,) + u2t.shape[1:], u2t.dtype)
    return pl.pallas_call(
        _dispatch_kernel,
        grid_spec=pltpu.PrefetchScalarGridSpec(
            num_scalar_prefetch=1,
            grid=(n // DISPATCH_TILE,),
            in_specs=[pl.BlockSpec(memory_space=pl.ANY), pl.BlockSpec(memory_space=pl.ANY)],
            out_specs=pl.BlockSpec(memory_space=pl.ANY),
            scratch_shapes=[pltpu.SemaphoreType.DMA(())]),
        out_shape=jax.ShapeDtypeStruct(init.shape, init.dtype),
        input_output_aliases={2: 0},
        compiler_params=_cparams(has_side_effects=True),
        name="moe_dispatch",
    )(dest_flat, u2t, init)


def _expert_kernel(blk_ref, nact_ref, x_ref, w1_ref, w3_ref, w2_ref, y_ref):
    @pl.when(pl.program_id(0) < nact_ref[0])
    def _():
        x = _rows_from_tiles(x_ref).astype(BF16)
        hid = _silu(_dot(x, w1_ref[...])) * _dot(x, w3_ref[...])
        y = _dot(hid.astype(BF16), w2_ref[...])
        for s in range(D_MODEL // LANES):
            y_ref[:, s, :] = y[:, s * LANES:(s + 1) * LANES]

    @pl.when(pl.program_id(0) >= nact_ref[0])
    def _():
        y_ref[...] = jnp.zeros_like(y_ref)


def _experts(blk_e, nact, xin, w1, w3, w2):
    nb = xin.shape[0] // EXPERT_BLOCK
    tile = (EXPERT_BLOCK, D_MODEL // LANES, LANES)
    return pl.pallas_call(
        _expert_kernel,
        grid_spec=pltpu.PrefetchScalarGridSpec(
            num_scalar_prefetch=2,
            grid=(nb,),
            in_specs=[pl.BlockSpec(tile, lambda i, blk, na: (i, 0, 0)),
                      pl.BlockSpec((None, D_MODEL, EXPERT_HIDDEN), lambda i, blk, na: (blk[i], 0, 0)),
                      pl.BlockSpec((None, D_MODEL, EXPERT_HIDDEN), lambda i, blk, na: (blk[i], 0, 0)),
                      pl.BlockSpec((None, EXPERT_HIDDEN, D_MODEL), lambda i, blk, na: (blk[i], 0, 0))],
            out_specs=pl.BlockSpec(tile, lambda i, blk, na: (i, 0, 0))),
        out_shape=jax.ShapeDtypeStruct(xin.shape, F32),
        compiler_params=_cparams(),
        name="moe_experts",
    )(blk_e, nact, xin, w1, w3, w2)


COMBINE_TILE = 256


def _combine_kernel(dest_ref, y_ref, meta_ref, h_ref, g2_ref, lng_ref, lnb_ref, o_ref, buf, sem, *, alpha):
    t = COMBINE_TILE
    base = pl.program_id(0) * t

    def copy(k, slot):
        return pltpu.make_async_copy(y_ref.at[dest_ref[TOP_K * (base + k) + slot]], buf.at[slot, k], sem)

    def issue(k, carry):
        for slot in range(TOP_K):
            copy(k, slot).start()
        return carry

    def drain(k, carry):
        for slot in range(TOP_K):
            copy(k, slot).wait()
        return carry

    lax.fori_loop(0, t, issue, 0)
    lax.fori_loop(0, t, drain, 0)
    meta = meta_ref[...]
    f = meta[:, 2:3] * _rows_from_tiles(buf.at[0]) + meta[:, 3:4] * _rows_from_tiles(buf.at[1])
    o_ref[...] = _layer_norm(alpha * h_ref[...] + g2_ref[...] * f) * lng_ref[...] + lnb_ref[...]


def _combine(dest_flat, yout, meta, h1, mods, ln_g, ln_b, tps, alpha):
    n = h1.shape[0]
    t = COMBINE_TILE
    d = D_MODEL

    def mod_map(i, dest):
        return (jnp.where(i % tps == 0, 8, i // tps), 0, 5)

    return pl.pallas_call(
        functools.partial(_combine_kernel, alpha=alpha),
        grid_spec=pltpu.PrefetchScalarGridSpec(
            num_scalar_prefetch=1,
            grid=(n // t,),
            in_specs=[pl.BlockSpec(memory_space=pl.ANY),
                      pl.BlockSpec((t, LANES), lambda i, dest: (i, 0)),
                      pl.BlockSpec((t, d), lambda i, dest: (i, 0)),
                      pl.BlockSpec((None, 1, d), mod_map),
                      pl.BlockSpec((1, d), lambda i, dest: (0, 0)),
                      pl.BlockSpec((1, d), lambda i, dest: (0, 0))],
            out_specs=pl.BlockSpec((t, d), lambda i, dest: (i, 0)),
            scratch_shapes=[pltpu.VMEM((TOP_K, t, d // LANES, LANES), F32), pltpu.SemaphoreType.DMA(())]),
        out_shape=jax.ShapeDtypeStruct((n, d), F32),
        compiler_params=_cparams(),
        name="moe_combine",
    )(dest_flat, yout, meta, h1, mods, ln_g, ln_b)


def _prep_w_in(w_in):
    cq, ckv, kr, z, xbc, dt, lx, lg, gate = _split_sections(w_in)
    zeros = lambda w: jnp.zeros(w_in.shape[:-1] + (w,), w_in.dtype)
    out = jnp.concatenate([z, lx, lg, gate, xbc, cq, ckv, kr, zeros(64),
                           dt[..., :SSD_HEADS], zeros(DT_DIR_STRIDE - SSD_HEADS),
                           dt[..., SSD_HEADS:], zeros(DT_DIR_STRIDE - SSD_HEADS)], axis=-1)
    return out.astype(BF16)


def _split_sections(w):
    sizes = (Q_LORA, KV_LORA, QK_ROPE, SSD_INNER, SSD_INNER + 2 * SSD_GROUPS * SSD_STATE, 2 * SSD_HEADS,
             LRU_WIDTH, LRU_WIDTH, N_BRANCH * D_MODEL)
    out, start = [], 0
    for size in sizes:
        out.append(w[..., start:start + size])
        start += size
    return out


def _dir_row(v):
    row = jnp.zeros((LANES,), F32)
    row = row.at[0:SSD_HEADS].set(v[0]).at[DT_DIR_STRIDE:DT_DIR_STRIDE + SSD_HEADS].set(v[1])
    return row[None, :]


def _expand_matrix(rev):
    rows = jnp.arange(LANES)[:, None]
    cols = jnp.arange(SSD_INNER)[None, :]
    base = DT_DIR_STRIDE if rev else 0
    return (rows - base == cols // SSD_HEADDIM).astype(BF16)


def _block_diag_gates(wa, wx):
    per = LRU_GROUP // LRU_BW
    eye = jnp.eye(per, dtype=wa.dtype)

    def bd(w):
        w = w.reshape(LRU_WIDTH // LRU_GROUP, per, LRU_BW, LRU_BW)
        return jnp.einsum('gicd,ij->gicjd', w, eye).reshape(LRU_WIDTH // LRU_GROUP, LRU_GROUP, LRU_GROUP)

    return jnp.concatenate([bd(wa), bd(wx)], axis=-1).astype(BF16)


def _rope_tables(n_ctx, seq):
    rows = seq // GRID_W
    row_pos = jnp.repeat(jnp.arange(rows, dtype=F32), GRID_W)
    col_pos = (jnp.arange(rows * GRID_W) % GRID_W).astype(F32)
    inv_freq = ROPE_THETA ** (-jnp.arange(ROPE_FREQS, dtype=F32) / ROPE_FREQS)
    ang = [row_pos[:, None] * inv_freq, col_pos[:, None] * inv_freq]
    cos = jnp.concatenate([jnp.cos(ang[0]), jnp.cos(ang[0]), jnp.cos(ang[1]), jnp.cos(ang[1])], axis=1)
    sin = jnp.concatenate([-jnp.sin(ang[0]), jnp.sin(ang[0]), -jnp.sin(ang[1]), jnp.sin(ang[1])], axis=1)
    pad = lambda t, fill: jnp.concatenate(
        [jnp.concatenate([jnp.full((n_ctx, QK_ROPE), fill, F32), t], axis=0),
         jnp.zeros((n_ctx + seq, LANES - QK_ROPE), F32)], axis=1)
    return pad(cos, 1.0), pad(sin, 0.0)


def kernel(x, c, ctx, c_ctx, w_mod, b_mod, w_in, q_norm_w, kv_norm_w, w_uq, w_ukv, ssd_conv_w, ssd_conv_b, ssd_a_log, ssd_dt_bias, ssd_d, ssd_norm_w, lru_conv_w, lru_conv_b, lru_wa, lru_ba, lru_wx, lru_bx, lru_lambda, w_branch, w_out, ln1_g, ln1_b, ln2_g, ln2_b, router_wg, router_bg, router_we, router_be, exp_w1, exp_w3, exp_w2):
    bsz, seq, d = x.shape
    n_ctx = ctx.shape[1]
    depth = w_mod.shape[0]
    assert d == D_MODEL and n_ctx == ROW_TILE and seq % ROW_TILE == 0 and bsz == SUBLANES
    s_all = n_ctx + seq
    tps = s_all // ROW_TILE
    n = bsz * s_all
    alpha = (2 * depth) ** 0.25

    h = jnp.concatenate([ctx, x], axis=1).reshape(n, d)
    cvec = jnp.zeros((16, d), F32).at[:bsz].set(c).at[bsz].set(c_ctx)
    mods_all = _mod_vectors(cvec, w_mod, b_mod).reshape(depth, 16, 1, 6 * d)
    cos_t, sin_t = _rope_tables(n_ctx, seq)
    expand = (_expand_matrix(False), _expand_matrix(True))

    w_in_p = _prep_w_in(w_in)
    wq = jnp.pad(w_uq.reshape(depth, Q_LORA, MLA_HEADS, QK_NOPE + QK_ROPE),
                 ((0, 0), (0, 0), (0, 0), (0, QK_PAD - QK_NOPE - QK_ROPE))).reshape(depth, Q_LORA, -1).astype(BF16)
    wkv4 = w_ukv.reshape(depth, KV_LORA, MLA_HEADS, QK_NOPE + V_DIM)
    wkv = jnp.concatenate([wkv4[..., :QK_NOPE].reshape(depth, KV_LORA, -1),
                           wkv4[..., QK_NOPE:].reshape(depth, KV_LORA, -1)], axis=-1).astype(BF16)
    w_router = jnp.concatenate([router_wg, router_we,
                                jnp.zeros((depth, d, LANES - N_GROUPS - N_EXPERTS), F32)], axis=-1)
    b_router = jnp.concatenate([router_bg, router_be,
                                jnp.zeros((depth, LANES - N_GROUPS - N_EXPERTS), F32)], axis=-1)
    wb_all = w_branch.astype(BF16)
    wo_all = w_out.astype(BF16)
    w1_all, w3_all, w2_all = exp_w1.astype(BF16), exp_w3.astype(BF16), exp_w2.astype(BF16)

    n_blocks = -(-(n * TOP_K) // EXPERT_BLOCK) + N_EXPERTS
    row1 = lambda v: v.reshape(1, -1)

    for l in range(depth):
        mods = mods_all[l]
        u1 = _ln_modulate(h, mods, tps)
        proj = _matmul(u1, w_in_p[l], 512, 1024)
        proj3 = proj.reshape(bsz, s_all, IN_COLS_PAD)

        q, k, v = _mla_prep(proj, row1(q_norm_w[l]), row1(kv_norm_w[l]), wq[l], wkv[l], cos_t, sin_t, tps)
        att = _attention(q.reshape(bsz, s_all, -1), k.reshape(bsz, s_all, -1), v.reshape(bsz, s_all, -1), n_ctx)

        xbc_conv = _dwconv(proj3, COL_XBC, 2 * SSD_INNER, ssd_conv_w[l], row1(ssd_conv_b[l]), True)
        dtb = _dir_row(ssd_dt_bias[l])
        aneg = _dir_row(-jnp.exp(ssd_a_log[l]))
        ys = [_ssd_scan(xbc_conv, proj3, dtb, aneg, expand[dr], n_ctx, bool(dr)) for dr in range(2)]

        lx_conv = _dwconv(proj3, COL_LX, LRU_WIDTH, lru_conv_w[l], row1(lru_conv_b[l]), False)
        hs = [_lru_scan(lx_conv, _block_diag_gates(lru_wa[l, dr], lru_wx[l, dr]), row1(lru_ba[l, dr]),
                        row1(lru_bx[l, dr]), row1(lru_lambda[l, dr]), n_ctx, bool(dr)) for dr in range(2)]

        d_skip_row = row1(jnp.repeat(ssd_d[l], SSD_HEADDIM))
        h1, u2t = _merge(att.reshape(n, -1), ys[0].reshape(n, -1), ys[1].reshape(n, -1),
                         xbc_conv.reshape(n, -1), proj, hs[0].reshape(n, -1), hs[1].reshape(n, -1), h,
                         d_skip_row, row1(ssd_norm_w[l]), wb_all[l], wo_all[l], mods,
                         row1(ln1_g[l]), row1(ln1_b[l]), tps, alpha)

        meta, cnt = _router(u2t, w_router[l], row1(b_router[l]))
        counts = cnt[0, :N_EXPERTS].astype(jnp.int32)
        padded = (counts + EXPERT_BLOCK - 1) // EXPERT_BLOCK * EXPERT_BLOCK
        pend = jnp.cumsum(padded)
        pstart = pend - padded
        eid = meta[:, 0:TOP_K].astype(jnp.int32)
        rank = meta[:, 4:4 + TOP_K].astype(jnp.int32)
        dest = (jnp.take(pstart, eid) + rank).reshape(-1)
        blk_e = jnp.minimum(jnp.searchsorted(pend, jnp.arange(n_blocks, dtype=jnp.int32) * EXPERT_BLOCK,
                                             side='right'), N_EXPERTS - 1).astype(jnp.int32)
        nact = (pend[-1:] // EXPERT_BLOCK).astype(jnp.int32)
        blk_e = jnp.where(jnp.arange(n_blocks) < nact[0], blk_e, jnp.take(blk_e, jnp.maximum(nact[0] - 1, 0)))

        xin = _dispatch(dest, u2t, n_blocks * EXPERT_BLOCK)
        yout = _experts(blk_e, nact, xin, w1_all[l], w3_all[l], w2_all[l])
        h = _combine(dest, yout, meta, h1, mods, row1(ln2_g[l]), row1(ln2_b[l]), tps, alpha)

    return h.reshape(bsz, s_all, d)[:, n_ctx:, :]
```

```python
import functools
import math

import jax
import jax.numpy as jnp
from jax import lax
from jax.experimental import pallas as pl
from jax.experimental.pallas import tpu as pltpu

F32 = jnp.float32
BF16 = jnp.bfloat16

D_MODEL = 1024
GRID_W = 64
N_BRANCH = 3

MLA_HEADS = 8
QK_NOPE = 128
QK_ROPE = 64
V_DIM = 128
Q_LORA = 512
KV_LORA = 256
ROPE_FREQS = QK_ROPE // 4
ROPE_THETA = 10000.0
ATTN_SCALE = (QK_NOPE + QK_ROPE) ** -0.5
QK_PAD = 256

SSD_HEADDIM = 64
SSD_INNER = 1024
SSD_HEADS = 16
SSD_GROUPS = 4
SSD_STATE = 128
SSD_CHUNK = 128
GROUP_W = SSD_INNER // SSD_GROUPS

CONV_W = 4
LRU_WIDTH = 1024
LRU_BW = 64
LRU_C = 8.0
LRU_TILE = 128
LRU_GROUP = 256

N_GROUPS = 4
EXPERTS_PER_GROUP = 8
N_EXPERTS = 32
TOP_K = 2
EXPERT_HIDDEN = 512
EXPERT_BLOCK = 128

LN_EPS = 1e-5
RMS_EPS = 1e-6

LANES = 128
SUBLANES = 8
ROW_TILE = 256
VMEM_LIMIT = 56 * 1024 * 1024

COL_Z = 0
COL_LX = 1024
COL_LG = 2048
COL_GATE = 3072
COL_XBC = 6144
COL_CQ = 8192
COL_CKV = 8704
COL_KR = 8960
COL_DT = 9088
DT_DIR_STRIDE = 64
IN_COLS_PAD = 9216


def _cparams(**kw):
    return pltpu.CompilerParams(vmem_limit_bytes=VMEM_LIMIT, **kw)


def _split3(x):
    hi = x.astype(BF16)
    r = x - hi.astype(F32)
    mid = r.astype(BF16)
    lo = (r - mid.astype(F32)).astype(BF16)
    return hi, mid, lo


def _dot(a, b):
    return jnp.dot(a, b, preferred_element_type=F32)


def _dot_nt(a, b):
    return lax.dot_general(a, b, (((1,), (1,)), ((), ())), preferred_element_type=F32)


def _sigmoid(x):
    return 1.0 / (1.0 + jnp.exp(-x))


def _silu(x):
    return x * _sigmoid(x)


def _softplus(x):
    return jnp.maximum(x, 0.0) + jnp.log(1.0 + jnp.exp(-jnp.abs(x)))


def _layer_norm(t):
    mu = jnp.mean(t, axis=-1, keepdims=True)
    c = t - mu
    var = jnp.mean(c * c, axis=-1, keepdims=True)
    return c * lax.rsqrt(var + LN_EPS)


def _mod_kernel(c_ref, w_ref, b_ref, o_ref):
    s = _silu(c_ref[...]).astype(BF16)
    o_ref[...] = _dot(s, w_ref[...].astype(BF16)) + b_ref[...]


def _mod_vectors(cvec, w_mod, b_mod):
    depth, d, n = w_mod.shape
    tn = 1536
    return pl.pallas_call(
        _mod_kernel,
        grid=(depth, n // tn),
        in_specs=[pl.BlockSpec((16, d), lambda l, j: (0, 0)),
                  pl.BlockSpec((None, d, tn), lambda l, j: (l, 0, j)),
                  pl.BlockSpec((None, 1, tn), lambda l, j: (l, 0, j))],
        out_specs=pl.BlockSpec((None, 16, tn), lambda l, j: (l, 0, j)),
        out_shape=jax.ShapeDtypeStruct((depth, 16, n), F32),
        compiler_params=_cparams(),
        name="mod_vectors",
    )(cvec, w_mod, b_mod.reshape(depth, 1, n))


def _mod_spec(tiles_per_sample, k):
    def imap(i):
        return (jnp.where(i % tiles_per_sample == 0, 8, i // tiles_per_sample), 0, k)
    return pl.BlockSpec((None, 1, D_MODEL), imap)


IN_PROJ_COLS = 1024


def _in_proj_kernel(h_ref, sh_ref, sc_ref, w_ref, o_ref):
    u = (_layer_norm(h_ref[...]) * (1.0 + sc_ref[...]) + sh_ref[...]).astype(BF16)
    for j in range(o_ref.shape[1] // IN_PROJ_COLS):
        cs = slice(j * IN_PROJ_COLS, (j + 1) * IN_PROJ_COLS)
        o_ref[:, cs] = _dot(u, w_ref[:, cs])


def _in_proj(h, mods, w, tps):
    n = h.shape[0]
    k, cols = w.shape
    return pl.pallas_call(
        _in_proj_kernel,
        grid=(n // ROW_TILE,),
        in_specs=[pl.BlockSpec((ROW_TILE, k), lambda i: (i, 0)),
                  _mod_spec(tps, 0), _mod_spec(tps, 1),
                  pl.BlockSpec((k, cols), lambda i: (0, 0), pipeline_mode=pl.Buffered(1))],
        out_specs=pl.BlockSpec((ROW_TILE, cols), lambda i: (i, 0)),
        out_shape=jax.ShapeDtypeStruct((n, cols), F32),
        compiler_params=_cparams(),
        name="in_proj",
    )(h, mods, mods, w)


def _rope128(t, cos, sin):
    lane = lax.broadcasted_iota(jnp.int32, t.shape, 1)
    partner = jnp.where(lane % (2 * ROPE_FREQS) < ROPE_FREQS,
                        pltpu.roll(t, LANES - ROPE_FREQS, 1), pltpu.roll(t, ROPE_FREQS, 1))
    return t * cos + partner * sin


def _rms(t, gain):
    return t * lax.rsqrt(jnp.mean(t * t, axis=-1, keepdims=True) + RMS_EPS) * gain


def _mla_prep_kernel(cq_ref, ckv_ref, kr_ref, qg_ref, kvg_ref, wq_ref, wkv_ref, cos_ref, sin_ref,
                     q_ref, k_ref, vt_ref):
    cos = cos_ref[...]
    sin = sin_ref[...]
    q = _dot(_rms(cq_ref[...], qg_ref[...]).astype(BF16), wq_ref[...]) * ATTN_SCALE
    kv = _dot(_rms(ckv_ref[...], kvg_ref[...]).astype(BF16), wkv_ref[...])
    krz = _rope128(kr_ref[...], cos, sin).astype(BF16)
    for h in range(MLA_HEADS):
        c0 = h * QK_PAD
        q_ref[:, c0:c0 + QK_NOPE] = q[:, c0:c0 + QK_NOPE].astype(BF16)
        q_ref[:, c0 + QK_NOPE:c0 + QK_PAD] = _rope128(q[:, c0 + QK_NOPE:c0 + QK_PAD], cos, sin).astype(BF16)
        k_ref[:, c0:c0 + QK_NOPE] = kv[:, h * QK_NOPE:(h + 1) * QK_NOPE].astype(BF16)
        k_ref[:, c0 + QK_NOPE:c0 + QK_PAD] = krz
    vt_ref[...] = kv[:, MLA_HEADS * QK_NOPE:].T.astype(BF16)


def _mla_prep(proj, q_gain, kv_gain, wq, wkv, cos_t, sin_t, tps):
    n = proj.shape[0]
    t = ROW_TILE
    row = lambda w, cb: pl.BlockSpec((t, w), lambda i: (i, cb))
    const = lambda shape: pl.BlockSpec(shape, lambda i: (0, 0))
    return pl.pallas_call(
        _mla_prep_kernel,
        grid=(n // t,),
        in_specs=[row(Q_LORA, COL_CQ // Q_LORA), row(KV_LORA, COL_CKV // KV_LORA), row(LANES, COL_KR // LANES),
                  const((1, Q_LORA)), const((1, KV_LORA)),
                  const((Q_LORA, MLA_HEADS * QK_PAD)), const((KV_LORA, 2 * MLA_HEADS * QK_NOPE)),
                  pl.BlockSpec((t, LANES), lambda i: (i % tps, 0)),
                  pl.BlockSpec((t, LANES), lambda i: (i % tps, 0))],
        out_specs=[pl.BlockSpec((t, MLA_HEADS * QK_PAD), lambda i: (i, 0)),
                   pl.BlockSpec((t, MLA_HEADS * QK_PAD), lambda i: (i, 0)),
                   pl.BlockSpec((None, MLA_HEADS * V_DIM, t), lambda i: (i // tps, 0, i % tps))],
        out_shape=[jax.ShapeDtypeStruct((n, MLA_HEADS * QK_PAD), BF16),
                   jax.ShapeDtypeStruct((n, MLA_HEADS * QK_PAD), BF16),
                   jax.ShapeDtypeStruct((n // (tps * t), MLA_HEADS * V_DIM, tps * t), BF16)],
        compiler_params=_cparams(),
        name="mla_prep",
    )(proj, proj, proj, q_gain, kv_gain, wq, wkv, cos_t, sin_t)


ATTN_KEY_CHUNK = 768


def _attn_kernel(q_ref, k_ref, vt_ref, o_ref, s0_ref, s1_ref, *, n_ctx):
    s_all = k_ref.shape[0]
    tq = ROW_TILE
    n_tiles = (s_all - n_ctx) // tq
    chunks = [(c, min(c + ATTN_KEY_CHUNK, s_all)) for c in range(0, s_all, ATTN_KEY_CHUNK)]

    def scores(s_ref, r0, key_chunks):
        q = q_ref[pl.ds(r0, tq), :]
        for c0, c1 in key_chunks:
            s_ref[c0:c1, :] = _dot_nt(k_ref[c0:c1, :], q)

    def finish(s_ref, r0, key_chunks):
        m = None
        for c0, c1 in key_chunks:
            cm = jnp.max(s_ref[c0:c1, :], axis=0, keepdims=True)
            m = cm if m is None else jnp.maximum(m, cm)
        l = acc = None
        for c0, c1 in key_chunks:
            p = jnp.exp(s_ref[c0:c1, :] - m)
            ps = jnp.sum(p, axis=0, keepdims=True)
            pv = _dot(vt_ref[:, c0:c1], p.astype(BF16))
            l, acc = (ps, pv) if l is None else (l + ps, acc + pv)
        o_ref[pl.ds(r0, tq), :] = (acc / l).T.astype(o_ref.dtype)

    scores(s0_ref, 0, [(0, n_ctx)])
    finish(s0_ref, 0, [(0, n_ctx)])
    scores(s0_ref, n_ctx, chunks)

    def body(j, carry):
        r_a = pl.multiple_of(n_ctx + 2 * j * tq, tq)
        r_b = pl.multiple_of(r_a + tq, tq)
        r_c = pl.multiple_of(jnp.minimum(r_b + tq, s_all - tq), tq)
        scores(s1_ref, r_b, chunks)
        finish(s0_ref, r_a, chunks)
        scores(s0_ref, r_c, chunks)
        finish(s1_ref, r_b, chunks)
        return carry

    assert n_tiles % 2 == 0
    lax.fori_loop(0, n_tiles // 2, body, 0)


def _attention(q, k, vt, n_ctx):
    b, s, _ = q.shape
    return pl.pallas_call(
        functools.partial(_attn_kernel, n_ctx=n_ctx),
        grid=(b, MLA_HEADS),
        in_specs=[pl.BlockSpec((None, s, QK_PAD), lambda bi, h: (bi, 0, h)),
                  pl.BlockSpec((None, s, QK_PAD), lambda bi, h: (bi, 0, h)),
                  pl.BlockSpec((None, V_DIM, s), lambda bi, h: (bi, h, 0))],
        out_specs=pl.BlockSpec((None, s, V_DIM), lambda bi, h: (bi, 0, h)),
        out_shape=jax.ShapeDtypeStruct((b, s, MLA_HEADS * V_DIM), BF16),
        scratch_shapes=[pltpu.VMEM((s, ROW_TILE), F32), pltpu.VMEM((s, ROW_TILE), F32)],
        compiler_params=_cparams(),
        name="attention",
    )(q, k, vt)


def _conv_kernel(x_ref, p_ref, n_ref, w_ref, b_ref, o_ref, *, silu, tps):
    j = pl.program_id(1)
    has_prev = (j > 1).astype(F32)
    has_next = jnp.logical_and(j > 0, j < tps - 1).astype(F32)
    xe = jnp.concatenate([p_ref[...] * has_prev, x_ref[...], n_ref[...] * has_next], axis=0)
    rows = xe.shape[0]
    t = x_ref.shape[0]
    w = w_ref[...]
    y = b_ref[...] + w[2:3, :] * x_ref[...]
    y = y + w[0:1, :] * pltpu.roll(xe, 2, 0)[SUBLANES:SUBLANES + t]
    y = y + w[1:2, :] * pltpu.roll(xe, 1, 0)[SUBLANES:SUBLANES + t]
    y = y + w[3:4, :] * pltpu.roll(xe, rows - 1, 0)[SUBLANES:SUBLANES + t]
    o_ref[...] = _silu(y) if silu else y


def _dwconv(proj3, col0, width, w, bias, silu):
    b, s, _ = proj3.shape
    t = ROW_TILE
    tc = 1024
    cb = col0 // tc
    tps = s // t
    hb = t // SUBLANES
    last = s // SUBLANES - 1
    return pl.pallas_call(
        functools.partial(_conv_kernel, silu=silu, tps=tps),
        grid=(b, tps, width // tc),
        in_specs=[pl.BlockSpec((None, t, tc), lambda bi, j, c: (bi, j, cb + c)),
                  pl.BlockSpec((None, SUBLANES, tc), lambda bi, j, c: (bi, jnp.maximum(j * hb - 1, 0), cb + c)),
                  pl.BlockSpec((None, SUBLANES, tc), lambda bi, j, c: (bi, jnp.minimum((j + 1) * hb, last), cb + c)),
                  pl.BlockSpec((CONV_W, tc), lambda bi, j, c: (0, c)),
                  pl.BlockSpec((1, tc), lambda bi, j, c: (0, c))],
        out_specs=pl.BlockSpec((None, t, tc), lambda bi, j, c: (bi, j, c)),
        out_shape=jax.ShapeDtypeStruct((b, s, width), F32),
        compiler_params=_cparams(),
        name="dwconv",
    )(proj3, proj3, proj3, w, bias)


def _ssd_kernel(xs_ref, b_ref, c_ref, dt_ref, dtb_ref, aneg_ref, e_ref, y_ref, h_ref, *, rev):
    @pl.when(pl.program_id(1) == 0)
    def _():
        h_ref[...] = jnp.zeros_like(h_ref)

    q = SSD_CHUNK
    dt = _softplus(dt_ref[...] + dtb_ref[...])
    a = dt * aneg_ref[...]
    row = lax.broadcasted_iota(jnp.int32, (q, q), 0)
    col = lax.broadcasted_iota(jnp.int32, (q, q), 1)
    tri = (col >= row) if rev else (col <= row)
    tri_b = jnp.where(tri, 1.0, 0.0).astype(BF16)
    a3 = _split3(a)
    acum = _dot(tri_b, a3[0]) + _dot(tri_b, a3[1]) + _dot(tri_b, a3[2])
    total = jnp.sum(a, axis=0, keepdims=True)
    stack = jnp.concatenate([dt, total - acum, acum, jnp.broadcast_to(total, (SUBLANES, LANES))], axis=0)
    e = e_ref[...]
    s3 = _split3(stack)
    ex = _dot(s3[0], e) + _dot(s3[1], e) + _dot(s3[2], e)
    dt_e = ex[0:q]
    to_end_e = jnp.exp(ex[q:2 * q])
    from_start_e = jnp.exp(ex[2 * q:3 * q])
    chunk_decay_e = jnp.exp(ex[3 * q:3 * q + 1])
    xg = xs_ref[...] * dt_e
    xg_b = xg.astype(BF16)
    w_b = (xg * to_end_e).astype(BF16)
    acum_t = acum.T
    base = DT_DIR_STRIDE if rev else 0
    for g in range(SSD_GROUPS):
        gs = slice(g * GROUP_W, (g + 1) * GROUP_W)
        bg = b_ref[:, g * SSD_STATE:(g + 1) * SSD_STATE]
        cg = c_ref[:, g * SSD_STATE:(g + 1) * SSD_STATE].astype(BF16)
        cb = _dot_nt(cg, bg.astype(BF16))
        h_prev = h_ref[:, gs]
        y_off = _dot(cg, h_prev.astype(BF16)) * from_start_e[:, gs]
        h_ref[:, gs] = chunk_decay_e[:, gs] * h_prev + _dot(bg.T.astype(BF16), w_b[:, gs])
        parts = []
        for hh in range(SSD_HEADS // SSD_GROUPS):
            head = g * (SSD_HEADS // SSD_GROUPS) + hh
            c = base + head
            seg = acum[:, c:c + 1] - acum_t[c:c + 1, :]
            lmat = jnp.exp(jnp.where(tri, seg, -1e30))
            parts.append(_dot((cb * lmat).astype(BF16), xg_b[:, head * SSD_HEADDIM:(head + 1) * SSD_HEADDIM]))
        y_ref[:, gs] = jnp.concatenate(parts, axis=1) + y_off


def _ssd_order(step, n_ctx_chunks, n_chunks, rev):
    if not rev:
        return step
    return jnp.where(step < n_ctx_chunks, n_ctx_chunks - 1 - step, n_chunks - 1 + n_ctx_chunks - step)


def _ssd_scan(xbc_conv, proj3, dt_bias_row, a_neg_row, expand, n_ctx, rev):
    b, s, _ = xbc_conv.shape
    q = SSD_CHUNK
    nchunks = s // q
    order = functools.partial(_ssd_order, n_ctx_chunks=n_ctx // q, n_chunks=nchunks, rev=rev)
    const = lambda shape: pl.BlockSpec(shape, lambda bi, i: (0, 0))
    return pl.pallas_call(
        functools.partial(_ssd_kernel, rev=rev),
        grid=(b, nchunks),
        in_specs=[pl.BlockSpec((None, q, SSD_INNER), lambda bi, i: (bi, order(i), 0)),
                  pl.BlockSpec((None, q, SSD_GROUPS * SSD_STATE), lambda bi, i: (bi, order(i), 2)),
                  pl.BlockSpec((None, q, SSD_GROUPS * SSD_STATE), lambda bi, i: (bi, order(i), 3)),
                  pl.BlockSpec((None, q, LANES), lambda bi, i: (bi, order(i), COL_DT // LANES)),
                  const((1, LANES)), const((1, LANES)), const((LANES, SSD_INNER))],
        out_specs=pl.BlockSpec((None, q, SSD_INNER), lambda bi, i: (bi, order(i), 0)),
        out_shape=jax.ShapeDtypeStruct((b, s, SSD_INNER), F32),
        scratch_shapes=[pltpu.VMEM((SSD_STATE, SSD_INNER), F32)],
        compiler_params=_cparams(),
        name="ssd_rev" if rev else "ssd_fwd",
    )(xbc_conv, xbc_conv, xbc_conv, proj3, dt_bias_row, a_neg_row, expand)


def _lru_kernel(x_ref, w_ref, ba_ref, bx_ref, lam_ref, o_ref, a_s, b_s, h_s, carry, *, rev):
    nb, t, _ = x_ref.shape
    per = LRU_GROUP // LANES

    @pl.when(pl.program_id(0) == 0)
    def _():
        carry[...] = jnp.zeros_like(carry)

    decay = -LRU_C * _softplus(-lam_ref[...])
    for bi in range(nb):
        for g in range(LRU_WIDTH // LRU_GROUP):
            gs = slice(g * LRU_GROUP, (g + 1) * LRU_GROUP)
            xd = x_ref[bi, :, gs]
            ri = _dot(xd.astype(BF16), w_ref[g])
            r = _sigmoid(ri[:, :LRU_GROUP] + ba_ref[:, gs])
            i = _sigmoid(ri[:, LRU_GROUP:] + bx_ref[:, gs])
            log_a = decay[:, gs] * r
            a = jnp.exp(log_a)
            bt = jnp.sqrt(jnp.tanh(-log_a) * (1.0 + a * a)) * (i * xd)
            for cc in range(per):
                ls = slice(cc * LANES, (cc + 1) * LANES)
                a_s[g * per + cc, pl.ds(bi, t, stride=nb), :] = a[:, ls]
                b_s[g * per + cc, pl.ds(bi, t, stride=nb), :] = bt[:, ls]

    def step(k, h):
        tt = (t - 1 - k) if rev else k
        r0 = pl.multiple_of(tt * nb, nb)
        h = a_s[:, pl.ds(r0, nb), :] * h + b_s[:, pl.ds(r0, nb), :]
        h_s[:, pl.ds(r0, nb), :] = h
        return h

    carry[...] = lax.fori_loop(0, t, step, carry[...])
    for bi in range(nb):
        for cc in range(LRU_WIDTH // LANES):
            o_ref[bi, :, cc * LANES:(cc + 1) * LANES] = h_s[cc, pl.ds(bi, t, stride=nb), :]


def _lru_scan(lx_conv, w_gate, ba, bx, lam, n_ctx, rev):
    b, s, width = lx_conv.shape
    t = LRU_TILE
    ntiles = s // t
    order = functools.partial(_ssd_order, n_ctx_chunks=n_ctx // t, n_chunks=ntiles, rev=rev)
    const = lambda shape: pl.BlockSpec(shape, lambda i: (0,) * len(shape))
    return pl.pallas_call(
        functools.partial(_lru_kernel, rev=rev),
        grid=(ntiles,),
        in_specs=[pl.BlockSpec((b, t, width), lambda i: (0, order(i), 0)),
                  const(w_gate.shape), const((1, width)), const((1, width)), const((1, width))],
        out_specs=pl.BlockSpec((b, t, width), lambda i: (0, order(i), 0)),
        out_shape=jax.ShapeDtypeStruct((b, s, width), F32),
        scratch_shapes=[pltpu.VMEM((width // LANES, t * b, LANES), F32)] * 3
        + [pltpu.VMEM((width // LANES, b, LANES), F32)],
        compiler_params=_cparams(),
        name="lru_rev" if rev else "lru_fwd",
    )(lx_conv, w_gate, ba, bx, lam)


def _gelu_tanh(x):
    return 0.5 * x * (1.0 + jnp.tanh(math.sqrt(2.0 / math.pi) * (x + 0.044715 * (x * x * x))))


def _merge_kernel(att_ref, y0_ref, y1_ref, xs_ref, z_ref, l0_ref, l1_ref, lg_ref, gate_ref, h_ref,
                  dskip_ref, ssdg_ref, wb_ref, wo_ref, g1_ref, lng_ref, lnb_ref, sh2_ref, sc2_ref,
                  h1_ref, u2_ref, *, alpha):
    y = y0_ref[...] + y1_ref[...] + xs_ref[...] * dskip_ref[...]
    y = y * _silu(z_ref[...])
    parts = []
    for g in range(SSD_GROUPS):
        yg = y[:, g * GROUP_W:(g + 1) * GROUP_W]
        parts.append(yg * lax.rsqrt(jnp.mean(yg * yg, axis=-1, keepdims=True) + RMS_EPS))
    y_ssd = jnp.concatenate(parts, axis=1) * ssdg_ref[...]
    y_lru = (l0_ref[...] + l1_ref[...]) * _gelu_tanh(lg_ref[...])
    branches = (att_ref[...], y_ssd.astype(BF16), y_lru.astype(BF16))
    mix = None
    for k in range(N_BRANCH):
        term = _sigmoid(gate_ref[:, k * D_MODEL:(k + 1) * D_MODEL]) * _dot(branches[k], wb_ref[k])
        mix = term if mix is None else mix + term
    out = _dot(mix.astype(BF16), wo_ref[...])
    h1 = _layer_norm(alpha * h_ref[...] + g1_ref[...] * out) * lng_ref[...] + lnb_ref[...]
    h1_ref[...] = h1
    u2 = _layer_norm(h1) * (1.0 + sc2_ref[...]) + sh2_ref[...]
    for s in range(D_MODEL // LANES):
        u2_ref[:, s, :] = u2[:, s * LANES:(s + 1) * LANES]


def _merge(att, y0, y1, xbc_conv, proj, l0, l1, h, d_skip_row, ssd_gain, wb, wo, mods, ln_g, ln_b, tps, alpha):
    n = h.shape[0]
    t = ROW_TILE
    d = D_MODEL
    row = lambda cb: pl.BlockSpec((t, d), lambda i: (i, cb))
    const = lambda shape: pl.BlockSpec(shape, lambda i: (0,) * len(shape))
    return pl.pallas_call(
        functools.partial(_merge_kernel, alpha=alpha),
        grid=(n // t,),
        in_specs=[row(0), row(0), row(0), row(0), row(COL_Z // d), row(0), row(0), row(COL_LG // d),
                  pl.BlockSpec((t, N_BRANCH * d), lambda i: (i, COL_GATE // (N_BRANCH * d))), row(0),
                  const((1, d)), const((1, d)), const(wb.shape), const(wo.shape),
                  _mod_spec(tps, 2), const((1, d)), const((1, d)), _mod_spec(tps, 3), _mod_spec(tps, 4)],
        out_specs=[row(0), pl.BlockSpec((t, d // LANES, LANES), lambda i: (i, 0, 0))],
        out_shape=[jax.ShapeDtypeStruct((n, d), F32),
                   jax.ShapeDtypeStruct((n, d // LANES, LANES), F32)],
        compiler_params=_cparams(),
        name="merge",
    )(att, y0, y1, xbc_conv, proj, l0, l1, proj, proj, h, d_skip_row, ssd_gain, wb, wo,
      mods, ln_g, ln_b, mods, mods)


def _rows_from_tiles(ref):
    return jnp.concatenate([ref[:, s, :] for s in range(ref.shape[1])], axis=1)


def _router_kernel(u_ref, w_ref, b_ref, meta_ref, cnt_ref, count):
    @pl.when(pl.program_id(0) == 0)
    def _():
        count[...] = jnp.zeros_like(count)

    u = _rows_from_tiles(u_ref)
    t = u.shape[0]
    uh = u.astype(BF16)
    ul = (u - uh.astype(F32)).astype(BF16)
    w = w_ref[...]
    wh = w.astype(BF16)
    wl = (w - wh.astype(F32)).astype(BF16)
    logits = _dot(uh, wh) + _dot(uh, wl) + _dot(ul, wh) + b_ref[...]
    lane = lax.broadcasted_iota(jnp.int32, logits.shape, 1)
    neg = -jnp.inf
    big = 4 * LANES

    def top1(vals):
        m = jnp.max(vals, axis=-1, keepdims=True)
        idx = jnp.min(jnp.where(vals == m, lane, big), axis=-1, keepdims=True)
        return m, idx

    glog = jnp.where(lane < N_GROUPS, logits, neg)
    gmax, gsel = top1(glog)
    gval = 1.0 / jnp.sum(jnp.exp(glog - gmax), axis=-1, keepdims=True)
    lo = N_GROUPS + gsel * EXPERTS_PER_GROUP
    elog = jnp.where(jnp.logical_and(lane >= lo, lane < lo + EXPERTS_PER_GROUP), logits, neg)
    v1, i1 = top1(elog)
    v2, i2 = top1(jnp.where(lane == i1, neg, elog))
    e21 = jnp.exp(v2 - v1)
    w1 = gval / (1.0 + e21)
    w2 = gval * e21 / (1.0 + e21)
    e1 = i1 - N_GROUPS
    e2 = i2 - N_GROUPS
    onehot = jnp.logical_or(lane == e1, lane == e2)
    oh = jnp.where(onehot, 1.0, 0.0)
    r = lax.broadcasted_iota(jnp.int32, (t, t), 0)
    c = lax.broadcasted_iota(jnp.int32, (t, t), 1)
    before = jnp.where(c < r, 1.0, 0.0).astype(BF16)
    prefix = _dot(before, oh.astype(BF16)) + count[0:1, :]
    rank1 = jnp.sum(jnp.where(lane == e1, prefix, 0.0), axis=-1, keepdims=True)
    rank2 = jnp.sum(jnp.where(lane == e2, prefix, 0.0), axis=-1, keepdims=True)
    meta = jnp.where(lane == 0, e1.astype(F32), 0.0)
    meta = jnp.where(lane == 1, e2.astype(F32), meta)
    meta = jnp.where(lane == 2, w1, meta)
    meta = jnp.where(lane == 3, w2, meta)
    meta = jnp.where(lane == 4, rank1, meta)
    meta = jnp.where(lane == 5, rank2, meta)
    meta_ref[...] = meta
    count[...] = count[...] + jnp.sum(oh, axis=0, keepdims=True)
    cnt_ref[...] = count[...]


def _router(u2t, w_router, b_router):
    n = u2t.shape[0]
    t = ROW_TILE
    return pl.pallas_call(
        _router_kernel,
        grid=(n // t,),
        in_specs=[pl.BlockSpec((t, D_MODEL // LANES, LANES), lambda i: (i, 0, 0)),
                  pl.BlockSpec((D_MODEL, LANES), lambda i: (0, 0)),
                  pl.BlockSpec((1, LANES), lambda i: (0, 0))],
        out_specs=[pl.BlockSpec((t, LANES), lambda i: (i, 0)),
                   pl.BlockSpec((SUBLANES, LANES), lambda i: (0, 0))],
        out_shape=[jax.ShapeDtypeStruct((n, LANES), F32),
                   jax.ShapeDtypeStruct((SUBLANES, LANES), F32)],
        scratch_shapes=[pltpu.VMEM((SUBLANES, LANES), F32)],
        compiler_params=_cparams(),
        name="router",
    )(u2t, w_router, b_router)


DISPATCH_TILE = 512


def _dispatch_kernel(dest_ref, u_ref, init_ref, x_ref, sem):
    del init_ref
    base = pl.program_id(0) * DISPATCH_TILE

    def copy(k, slot):
        return pltpu.make_async_copy(u_ref.at[k], x_ref.at[dest_ref[TOP_K * (base + k) + slot]], sem)

    def issue(k, carry):
        for slot in range(TOP_K):
            copy(k, slot).start()
        return carry

    def drain(k, carry):
        for slot in range(TOP_K):
            copy(k, slot).wait()
        return carry

    lax.fori_loop(0, DISPATCH_TILE, issue, 0)
    lax.fori_loop(0, DISPATCH_TILE, drain, 0)


def _dispatch(dest_flat, u2t, n_rows):
    n = u2t.shape[0]
    init = jnp.zeros((n_rows,) + u2t.shape[1:], u2t.dtype)
    return pl.pallas_call(
        _dispatch_kernel,
        grid_spec=pltpu.PrefetchScalarGridSpec(
            num_scalar_prefetch=1,
            grid=(n // DISPATCH_TILE,),
            in_specs=[pl.BlockSpec((DISPATCH_TILE,) + u2t.shape[1:], lambda i, dest: (i, 0, 0)),
                      pl.BlockSpec(memory_space=pl.ANY)],
            out_specs=pl.BlockSpec(memory_space=pl.ANY),
            scratch_shapes=[pltpu.SemaphoreType.DMA(())]),
        out_shape=jax.ShapeDtypeStruct(init.shape, init.dtype),
        input_output_aliases={2: 0},
        compiler_params=_cparams(has_side_effects=True),
        name="moe_dispatch",
    )(dest_flat, u2t, init)


def _expert_kernel(blk_ref, nact_ref, x_ref, w1_ref, w3_ref, w2_ref, y_ref):
    @pl.when(pl.program_id(0) < nact_ref[0])
    def _():
        x = _rows_from_tiles(x_ref).astype(BF16)
        hid = _silu(_dot(x, w1_ref[...])) * _dot(x, w3_ref[...])
        y = _dot(hid.astype(BF16), w2_ref[...])
        for s in range(D_MODEL // LANES):
            y_ref[:, s, :] = y[:, s * LANES:(s + 1) * LANES]

    @pl.when(pl.program_id(0) >= nact_ref[0])
    def _():
        y_ref[...] = jnp.zeros_like(y_ref)


def _experts(blk_e, nact, xin, w1, w3, w2):
    nb = xin.shape[0] // EXPERT_BLOCK
    tile = (EXPERT_BLOCK, D_MODEL // LANES, LANES)
    return pl.pallas_call(
        _expert_kernel,
        grid_spec=pltpu.PrefetchScalarGridSpec(
            num_scalar_prefetch=2,
            grid=(nb,),
            in_specs=[pl.BlockSpec(tile, lambda i, blk, na: (i, 0, 0)),
                      pl.BlockSpec((None, D_MODEL, EXPERT_HIDDEN), lambda i, blk, na: (blk[i], 0, 0)),
                      pl.BlockSpec((None, D_MODEL, EXPERT_HIDDEN), lambda i, blk, na: (blk[i], 0, 0)),
                      pl.BlockSpec((None, EXPERT_HIDDEN, D_MODEL), lambda i, blk, na: (blk[i], 0, 0))],
            out_specs=pl.BlockSpec(tile, lambda i, blk, na: (i, 0, 0))),
        out_shape=jax.ShapeDtypeStruct(xin.shape, F32),
        compiler_params=_cparams(),
        name="moe_experts",
    )(blk_e, nact, xin, w1, w3, w2)


COMBINE_TILE = 256


def _combine_kernel(dest_ref, y_ref, meta_ref, h_ref, g2_ref, lng_ref, lnb_ref, o_ref, buf, sem, *, alpha):
    t = COMBINE_TILE
    base = pl.program_id(0) * t

    def copy(k, slot):
        return pltpu.make_async_copy(y_ref.at[dest_ref[TOP_K * (base + k) + slot]], buf.at[slot, k], sem)

    def issue(k, carry):
        for slot in range(TOP_K):
            copy(k, slot).start()
        return carry

    def drain(k, carry):
        for slot in range(TOP_K):
            copy(k, slot).wait()
        return carry

    lax.fori_loop(0, t, issue, 0)
    lax.fori_loop(0, t, drain, 0)
    meta = meta_ref[...]
    f = meta[:, 2:3] * _rows_from_tiles(buf.at[0]) + meta[:, 3:4] * _rows_from_tiles(buf.at[1])
    o_ref[...] = _layer_norm(alpha * h_ref[...] + g2_ref[...] * f) * lng_ref[...] + lnb_ref[...]


def _combine(dest_flat, yout, meta, h1, mods, ln_g, ln_b, tps, alpha):
    n = h1.shape[0]
    t = COMBINE_TILE
    d = D_MODEL

    def mod_map(i, dest):
        return (jnp.where(i % tps == 0, 8, i // tps), 0, 5)

    return pl.pallas_call(
        functools.partial(_combine_kernel, alpha=alpha),
        grid_spec=pltpu.PrefetchScalarGridSpec(
            num_scalar_prefetch=1,
            grid=(n // t,),
            in_specs=[pl.BlockSpec(memory_space=pl.ANY),
                      pl.BlockSpec((t, LANES), lambda i, dest: (i, 0)),
                      pl.BlockSpec((t, d), lambda i, dest: (i, 0)),
                      pl.BlockSpec((None, 1, d), mod_map),
                      pl.BlockSpec((1, d), lambda i, dest: (0, 0)),
                      pl.BlockSpec((1, d), lambda i, dest: (0, 0))],
            out_specs=pl.BlockSpec((t, d), lambda i, dest: (i, 0)),
            scratch_shapes=[pltpu.VMEM((TOP_K, t, d // LANES, LANES), F32), pltpu.SemaphoreType.DMA(())]),
        out_shape=jax.ShapeDtypeStruct((n, d), F32),
        compiler_params=_cparams(),
        name="moe_combine",
    )(dest_flat, yout, meta, h1, mods, ln_g, ln_b)


def _prep_w_in(w_in):
    cq, ckv, kr, z, xbc, dt, lx, lg, gate = _split_sections(w_in)
    zeros = lambda w: jnp.zeros(w_in.shape[:-1] + (w,), w_in.dtype)
    out = jnp.concatenate([z, lx, lg, gate, xbc, cq, ckv, kr, zeros(64),
                           dt[..., :SSD_HEADS], zeros(DT_DIR_STRIDE - SSD_HEADS),
                           dt[..., SSD_HEADS:], zeros(DT_DIR_STRIDE - SSD_HEADS)], axis=-1)
    return out.astype(BF16)


def _split_sections(w):
    sizes = (Q_LORA, KV_LORA, QK_ROPE, SSD_INNER, SSD_INNER + 2 * SSD_GROUPS * SSD_STATE, 2 * SSD_HEADS,
             LRU_WIDTH, LRU_WIDTH, N_BRANCH * D_MODEL)
    out, start = [], 0
    for size in sizes:
        out.append(w[..., start:start + size])
        start += size
    return out


def _dir_row(v):
    row = jnp.zeros((LANES,), F32)
    row = row.at[0:SSD_HEADS].set(v[0]).at[DT_DIR_STRIDE:DT_DIR_STRIDE + SSD_HEADS].set(v[1])
    return row[None, :]


def _expand_matrix(rev):
    rows = jnp.arange(LANES)[:, None]
    cols = jnp.arange(SSD_INNER)[None, :]
    base = DT_DIR_STRIDE if rev else 0
    return (rows - base == cols // SSD_HEADDIM).astype(BF16)


def _block_diag_gates(wa, wx):
    per = LRU_GROUP // LRU_BW
    eye = jnp.eye(per, dtype=wa.dtype)

    def bd(w):
        w = w.reshape(LRU_WIDTH // LRU_GROUP, per, LRU_BW, LRU_BW)
        return jnp.einsum('gicd,ij->gicjd', w, eye).reshape(LRU_WIDTH // LRU_GROUP, LRU_GROUP, LRU_GROUP)

    return jnp.concatenate([bd(wa), bd(wx)], axis=-1).astype(BF16)


def _rope_tables(n_ctx, seq):
    rows = seq // GRID_W
    row_pos = jnp.repeat(jnp.arange(rows, dtype=F32), GRID_W)
    col_pos = (jnp.arange(rows * GRID_W) % GRID_W).astype(F32)
    inv_freq = ROPE_THETA ** (-jnp.arange(ROPE_FREQS, dtype=F32) / ROPE_FREQS)
    ang = [row_pos[:, None] * inv_freq, col_pos[:, None] * inv_freq]
    cos = jnp.concatenate([jnp.cos(ang[0]), jnp.cos(ang[0]), jnp.cos(ang[1]), jnp.cos(ang[1])], axis=1)
    sin = jnp.concatenate([-jnp.sin(ang[0]), jnp.sin(ang[0]), -jnp.sin(ang[1]), jnp.sin(ang[1])], axis=1)
    pad = lambda t, fill: jnp.concatenate(
        [jnp.concatenate([jnp.full((n_ctx, QK_ROPE), fill, F32), t], axis=0),
         jnp.zeros((n_ctx + seq, LANES - QK_ROPE), F32)], axis=1)
    return pad(cos, 1.0), pad(sin, 0.0)


def kernel(x, c, ctx, c_ctx, w_mod, b_mod, w_in, q_norm_w, kv_norm_w, w_uq, w_ukv, ssd_conv_w, ssd_conv_b, ssd_a_log, ssd_dt_bias, ssd_d, ssd_norm_w, lru_conv_w, lru_conv_b, lru_wa, lru_ba, lru_wx, lru_bx, lru_lambda, w_branch, w_out, ln1_g, ln1_b, ln2_g, ln2_b, router_wg, router_bg, router_we, router_be, exp_w1, exp_w3, exp_w2):
    bsz, seq, d = x.shape
    n_ctx = ctx.shape[1]
    depth = w_mod.shape[0]
    assert d == D_MODEL and n_ctx == ROW_TILE and seq % ROW_TILE == 0 and bsz == SUBLANES
    s_all = n_ctx + seq
    tps = s_all // ROW_TILE
    n = bsz * s_all
    alpha = (2 * depth) ** 0.25

    h = jnp.concatenate([ctx, x], axis=1).reshape(n, d)
    cvec = jnp.zeros((16, d), F32).at[:bsz].set(c).at[bsz].set(c_ctx)
    mods_all = _mod_vectors(cvec, w_mod, b_mod).reshape(depth, 16, 1, 6 * d)
    cos_t, sin_t = _rope_tables(n_ctx, seq)
    expand = (_expand_matrix(False), _expand_matrix(True))

    w_in_p = _prep_w_in(w_in)
    wq = jnp.pad(w_uq.reshape(depth, Q_LORA, MLA_HEADS, QK_NOPE + QK_ROPE),
                 ((0, 0), (0, 0), (0, 0), (0, QK_PAD - QK_NOPE - QK_ROPE))).reshape(depth, Q_LORA, -1).astype(BF16)
    wkv4 = w_ukv.reshape(depth, KV_LORA, MLA_HEADS, QK_NOPE + V_DIM)
    wkv = jnp.concatenate([wkv4[..., :QK_NOPE].reshape(depth, KV_LORA, -1),
                           wkv4[..., QK_NOPE:].reshape(depth, KV_LORA, -1)], axis=-1).astype(BF16)
    w_router = jnp.concatenate([router_wg, router_we,
                                jnp.zeros((depth, d, LANES - N_GROUPS - N_EXPERTS), F32)], axis=-1)
    b_router = jnp.concatenate([router_bg, router_be,
                                jnp.zeros((depth, LANES - N_GROUPS - N_EXPERTS), F32)], axis=-1)
    wb_all = w_branch.astype(BF16)
    wo_all = w_out.astype(BF16)
    w1_all, w3_all, w2_all = exp_w1.astype(BF16), exp_w3.astype(BF16), exp_w2.astype(BF16)

    n_blocks = -(-(n * TOP_K) // EXPERT_BLOCK) + N_EXPERTS
    row1 = lambda v: v.reshape(1, -1)

    for l in range(depth):
        mods = mods_all[l]
        proj = _in_proj(h, mods, w_in_p[l], tps)
        proj3 = proj.reshape(bsz, s_all, IN_COLS_PAD)

        q, k, vt = _mla_prep(proj, row1(q_norm_w[l]), row1(kv_norm_w[l]), wq[l], wkv[l], cos_t, sin_t, tps)
        att = _attention(q.reshape(bsz, s_all, -1), k.reshape(bsz, s_all, -1), vt, n_ctx)

        xbc_conv = _dwconv(proj3, COL_XBC, 2 * SSD_INNER, ssd_conv_w[l], row1(ssd_conv_b[l]), True)
        dtb = _dir_row(ssd_dt_bias[l])
        aneg = _dir_row(-jnp.exp(ssd_a_log[l]))
        ys = [_ssd_scan(xbc_conv, proj3, dtb, aneg, expand[dr], n_ctx, bool(dr)) for dr in range(2)]

        lx_conv = _dwconv(proj3, COL_LX, LRU_WIDTH, lru_conv_w[l], row1(lru_conv_b[l]), False)
        hs = [_lru_scan(lx_conv, _block_diag_gates(lru_wa[l, dr], lru_wx[l, dr]), row1(lru_ba[l, dr]),
                        row1(lru_bx[l, dr]), row1(lru_lambda[l, dr]), n_ctx, bool(dr)) for dr in range(2)]

        d_skip_row = row1(jnp.repeat(ssd_d[l], SSD_HEADDIM))
        h1, u2t = _merge(att.reshape(n, -1), ys[0].reshape(n, -1), ys[1].reshape(n, -1),
                         xbc_conv.reshape(n, -1), proj, hs[0].reshape(n, -1), hs[1].reshape(n, -1), h,
                         d_skip_row, row1(ssd_norm_w[l]), wb_all[l], wo_all[l], mods,
                         row1(ln1_g[l]), row1(ln1_b[l]), tps, alpha)

        meta, cnt = _router(u2t, w_router[l], row1(b_router[l]))
        counts = cnt[0, :N_EXPERTS].astype(jnp.int32)
        padded = (counts + EXPERT_BLOCK - 1) // EXPERT_BLOCK * EXPERT_BLOCK
        pend = jnp.cumsum(padded)
        pstart = pend - padded
        eid = meta[:, 0:TOP_K].astype(jnp.int32)
        rank = meta[:, 4:4 + TOP_K].astype(jnp.int32)
        dest = (jnp.take(pstart, eid) + rank).reshape(-1)
        blk_start = jnp.arange(n_blocks, dtype=jnp.int32) * EXPERT_BLOCK
        blk_e = jnp.minimum(jnp.sum(pend[None, :] <= blk_start[:, None], axis=1), N_EXPERTS - 1).astype(jnp.int32)
        nact = (pend[-1:] // EXPERT_BLOCK).astype(jnp.int32)
        blk_e = jnp.where(jnp.arange(n_blocks) < nact[0], blk_e, jnp.take(blk_e, jnp.maximum(nact[0] - 1, 0)))

        xin = _dispatch(dest, u2t, n_blocks * EXPERT_BLOCK)
        yout = _experts(blk_e, nact, xin, w1_all[l], w3_all[l], w2_all[l])
        h = _combine(dest, yout, meta, h1, mods, row1(ln2_g[l]), row1(ln2_b[l]), tps, alpha)

    return h.reshape(bsz, s_all, d)[:, n_ctx:, :]
```

```python
import functools
import math

import jax
import jax.numpy as jnp
from jax import lax
from jax.experimental import pallas as pl
from jax.experimental.pallas import tpu as pltpu

F32 = jnp.float32
BF16 = jnp.bfloat16

D_MODEL = 1024
GRID_W = 64
N_BRANCH = 3

MLA_HEADS = 8
QK_NOPE = 128
QK_ROPE = 64
V_DIM = 128
Q_LORA = 512
KV_LORA = 256
ROPE_FREQS = QK_ROPE // 4
ROPE_THETA = 10000.0
ATTN_SCALE = (QK_NOPE + QK_ROPE) ** -0.5
QK_PAD = 256

SSD_HEADDIM = 64
SSD_INNER = 1024
SSD_HEADS = 16
SSD_GROUPS = 4
SSD_STATE = 128
SSD_CHUNK = 128
GROUP_W = SSD_INNER // SSD_GROUPS

CONV_W = 4
LRU_WIDTH = 1024
LRU_BW = 64
LRU_C = 8.0
LRU_TILE = 128
LRU_GROUP = 256

N_GROUPS = 4
EXPERTS_PER_GROUP = 8
N_EXPERTS = 32
TOP_K = 2
EXPERT_HIDDEN = 512
EXPERT_BLOCK = 128

LN_EPS = 1e-5
RMS_EPS = 1e-6

LANES = 128
SUBLANES = 8
ROW_TILE = 256
VMEM_LIMIT = 56 * 1024 * 1024

COL_Z = 0
COL_LX = 1024
COL_LG = 2048
COL_GATE = 3072
COL_XBC = 6144
COL_CQ = 8192
COL_CKV = 8704
COL_KR = 8960
COL_DT = 9088
DT_DIR_STRIDE = 64
IN_COLS_PAD = 9216


def _cparams(**kw):
    return pltpu.CompilerParams(vmem_limit_bytes=VMEM_LIMIT, **kw)


def _split3(x):
    hi = x.astype(BF16)
    r = x - hi.astype(F32)
    mid = r.astype(BF16)
    lo = (r - mid.astype(F32)).astype(BF16)
    return hi, mid, lo


def _dot(a, b):
    return jnp.dot(a, b, preferred_element_type=F32)


def _dot_nt(a, b):
    return lax.dot_general(a, b, (((1,), (1,)), ((), ())), preferred_element_type=F32)


def _sigmoid(x):
    return 1.0 / (1.0 + jnp.exp(-x))


def _sigmoid_tanh(x):
    return 0.5 * jnp.tanh(0.5 * x) + 0.5


def _silu(x):
    return x * _sigmoid(x)


def _softplus(x):
    return jnp.maximum(x, 0.0) + jnp.log(1.0 + jnp.exp(-jnp.abs(x)))


def _layer_norm(t):
    mu = jnp.mean(t, axis=-1, keepdims=True)
    c = t - mu
    var = jnp.mean(c * c, axis=-1, keepdims=True)
    return c * lax.rsqrt(var + LN_EPS)


def _mod_kernel(c_ref, w_ref, b_ref, o_ref):
    s = _silu(c_ref[...]).astype(BF16)
    o_ref[...] = _dot(s, w_ref[...].astype(BF16)) + b_ref[...]


def _mod_vectors(cvec, w_mod, b_mod):
    depth, d, n = w_mod.shape
    tn = 1536
    return pl.pallas_call(
        _mod_kernel,
        grid=(depth, n // tn),
        in_specs=[pl.BlockSpec((16, d), lambda l, j: (0, 0)),
                  pl.BlockSpec((None, d, tn), lambda l, j: (l, 0, j)),
                  pl.BlockSpec((None, 1, tn), lambda l, j: (l, 0, j))],
        out_specs=pl.BlockSpec((None, 16, tn), lambda l, j: (l, 0, j)),
        out_shape=jax.ShapeDtypeStruct((depth, 16, n), F32),
        compiler_params=_cparams(),
        name="mod_vectors",
    )(cvec, w_mod, b_mod.reshape(depth, 1, n))


def _mod_spec(tiles_per_sample, k):
    def imap(i):
        return (jnp.where(i % tiles_per_sample == 0, 8, i // tiles_per_sample), 0, k)
    return pl.BlockSpec((None, 1, D_MODEL), imap)


IN_PROJ_COLS = 1024


def _in_proj_kernel(h_ref, sh_ref, sc_ref, w_ref, o_ref, misc_ref):
    u = (_layer_norm(h_ref[...]) * (1.0 + sc_ref[...]) + sh_ref[...]).astype(BF16)
    main = o_ref.shape[1]
    for c0 in range(0, main, IN_PROJ_COLS):
        cs = slice(c0, min(c0 + IN_PROJ_COLS, main))
        o_ref[:, cs] = _dot(u, w_ref[:, cs]).astype(o_ref.dtype)
    misc_ref[...] = _dot(u, w_ref[:, main:])


def _in_proj(h, mods, w_all, layer, tps):
    n = h.shape[0]
    _, k, cols = w_all.shape
    return pl.pallas_call(
        _in_proj_kernel,
        grid=(n // ROW_TILE,),
        in_specs=[pl.BlockSpec((ROW_TILE, k), lambda i: (i, 0)),
                  _mod_spec(tps, 0), _mod_spec(tps, 1),
                  pl.BlockSpec((None, k, cols), lambda i: (layer, 0, 0), pipeline_mode=pl.Buffered(1))],
        out_specs=[pl.BlockSpec((ROW_TILE, COL_KR), lambda i: (i, 0)),
                   pl.BlockSpec((ROW_TILE, cols - COL_KR), lambda i: (i, 0))],
        out_shape=[jax.ShapeDtypeStruct((n, COL_KR), BF16),
                   jax.ShapeDtypeStruct((n, cols - COL_KR), F32)],
        compiler_params=_cparams(),
        name="in_proj",
    )(h, mods, mods, w_all)


def _rope128(t, cos, sin):
    lane = lax.broadcasted_iota(jnp.int32, t.shape, 1)
    partner = jnp.where(lane % (2 * ROPE_FREQS) < ROPE_FREQS,
                        pltpu.roll(t, LANES - ROPE_FREQS, 1), pltpu.roll(t, ROPE_FREQS, 1))
    return t * cos + partner * sin


def _rms(t, gain):
    return t * lax.rsqrt(jnp.mean(t * t, axis=-1, keepdims=True) + RMS_EPS) * gain


def _mla_prep_kernel(cq_ref, ckv_ref, kr_ref, qg_ref, kvg_ref, wq_ref, wkv_ref, cos_ref, sin_ref,
                     q_ref, k_ref, vt_ref):
    cos = cos_ref[...]
    sin = sin_ref[...]
    q = _dot(_rms(cq_ref[...].astype(F32), qg_ref[...]).astype(BF16), wq_ref[...]) * ATTN_SCALE
    kv = _dot(_rms(ckv_ref[...].astype(F32), kvg_ref[...]).astype(BF16), wkv_ref[...])
    krz = _rope128(kr_ref[...], cos, sin).astype(BF16)
    for h in range(MLA_HEADS):
        c0 = h * QK_PAD
        q_ref[:, c0:c0 + QK_NOPE] = q[:, c0:c0 + QK_NOPE].astype(BF16)
        q_ref[:, c0 + QK_NOPE:c0 + QK_PAD] = _rope128(q[:, c0 + QK_NOPE:c0 + QK_PAD], cos, sin).astype(BF16)
        k_ref[:, c0:c0 + QK_NOPE] = kv[:, h * QK_NOPE:(h + 1) * QK_NOPE].astype(BF16)
        k_ref[:, c0 + QK_NOPE:c0 + QK_PAD] = krz
    vt_ref[...] = kv[:, MLA_HEADS * QK_NOPE:].T.astype(BF16)


def _mla_prep(proj, misc, q_gain, kv_gain, wq_all, wkv_all, layer, cos_t, sin_t, tps):
    n = proj.shape[0]
    t = ROW_TILE
    row = lambda w, cb: pl.BlockSpec((t, w), lambda i: (i, cb))
    const = lambda shape: pl.BlockSpec(shape, lambda i: (0, 0))
    stacked = lambda shape: pl.BlockSpec((None,) + shape, lambda i: (layer, 0, 0))
    return pl.pallas_call(
        _mla_prep_kernel,
        grid=(n // t,),
        in_specs=[row(Q_LORA, COL_CQ // Q_LORA), row(KV_LORA, COL_CKV // KV_LORA), row(LANES, 0),
                  const((1, Q_LORA)), const((1, KV_LORA)),
                  stacked((Q_LORA, MLA_HEADS * QK_PAD)), stacked((KV_LORA, 2 * MLA_HEADS * QK_NOPE)),
                  pl.BlockSpec((t, LANES), lambda i: (i % tps, 0)),
                  pl.BlockSpec((t, LANES), lambda i: (i % tps, 0))],
        out_specs=[pl.BlockSpec((t, MLA_HEADS * QK_PAD), lambda i: (i, 0)),
                   pl.BlockSpec((t, MLA_HEADS * QK_PAD), lambda i: (i, 0)),
                   pl.BlockSpec((None, MLA_HEADS * V_DIM, t), lambda i: (i // tps, 0, i % tps))],
        out_shape=[jax.ShapeDtypeStruct((n, MLA_HEADS * QK_PAD), BF16),
                   jax.ShapeDtypeStruct((n, MLA_HEADS * QK_PAD), BF16),
                   jax.ShapeDtypeStruct((n // (tps * t), MLA_HEADS * V_DIM, tps * t), BF16)],
        compiler_params=_cparams(),
        name="mla_prep",
    )(proj, proj, misc, q_gain, kv_gain, wq_all, wkv_all, cos_t, sin_t)


ATTN_KEY_CHUNK = 768


def _attn_kernel(q_ref, k_ref, vt_ref, o_ref, s0_ref, s1_ref, *, n_ctx):
    s_all = k_ref.shape[0]
    tq = ROW_TILE
    n_tiles = (s_all - n_ctx) // tq
    chunks = [(c, min(c + ATTN_KEY_CHUNK, s_all)) for c in range(0, s_all, ATTN_KEY_CHUNK)]

    def scores(s_ref, r0, key_chunks):
        q = q_ref[pl.ds(r0, tq), :]
        for c0, c1 in key_chunks:
            s_ref[c0:c1, :] = _dot_nt(k_ref[c0:c1, :], q)

    def finish(s_ref, r0, key_chunks):
        m = None
        for c0, c1 in key_chunks:
            cm = jnp.max(s_ref[c0:c1, :], axis=0, keepdims=True)
            m = cm if m is None else jnp.maximum(m, cm)
        l = acc = None
        for c0, c1 in key_chunks:
            p = jnp.exp(s_ref[c0:c1, :] - m)
            ps = jnp.sum(p, axis=0, keepdims=True)
            pv = _dot(vt_ref[:, c0:c1], p.astype(BF16))
            l, acc = (ps, pv) if l is None else (l + ps, acc + pv)
        o_ref[pl.ds(r0, tq), :] = (acc / l).T.astype(o_ref.dtype)

    scores(s0_ref, 0, [(0, n_ctx)])
    finish(s0_ref, 0, [(0, n_ctx)])
    scores(s0_ref, n_ctx, chunks)

    def body(j, carry):
        r_a = pl.multiple_of(n_ctx + 2 * j * tq, tq)
        r_b = pl.multiple_of(r_a + tq, tq)
        r_c = pl.multiple_of(jnp.minimum(r_b + tq, s_all - tq), tq)
        scores(s1_ref, r_b, chunks)
        finish(s0_ref, r_a, chunks)
        scores(s0_ref, r_c, chunks)
        finish(s1_ref, r_b, chunks)
        return carry

    assert n_tiles % 2 == 0
    lax.fori_loop(0, n_tiles // 2, body, 0)


def _attention(q, k, vt, n_ctx):
    b, s, _ = q.shape
    return pl.pallas_call(
        functools.partial(_attn_kernel, n_ctx=n_ctx),
        grid=(b, MLA_HEADS),
        in_specs=[pl.BlockSpec((None, s, QK_PAD), lambda bi, h: (bi, 0, h)),
                  pl.BlockSpec((None, s, QK_PAD), lambda bi, h: (bi, 0, h)),
                  pl.BlockSpec((None, V_DIM, s), lambda bi, h: (bi, h, 0))],
        out_specs=pl.BlockSpec((None, s, V_DIM), lambda bi, h: (bi, 0, h)),
        out_shape=jax.ShapeDtypeStruct((b, s, MLA_HEADS * V_DIM), BF16),
        scratch_shapes=[pltpu.VMEM((s, ROW_TILE), F32), pltpu.VMEM((s, ROW_TILE), F32)],
        compiler_params=_cparams(),
        name="attention",
    )(q, k, vt)


def _conv_kernel(x_ref, p_ref, n_ref, w_ref, b_ref, o_ref, *, silu, tps):
    j = pl.program_id(1)
    has_prev = (j > 1).astype(F32)
    has_next = jnp.logical_and(j > 0, j < tps - 1).astype(F32)
    x = x_ref[...].astype(F32)
    xe = jnp.concatenate([p_ref[...].astype(F32) * has_prev, x, n_ref[...].astype(F32) * has_next], axis=0)
    rows = xe.shape[0]
    t = x_ref.shape[0]
    w = w_ref[...]
    y = b_ref[...] + w[2:3, :] * x
    y = y + w[0:1, :] * pltpu.roll(xe, 2, 0)[CONV_HALO:CONV_HALO + t]
    y = y + w[1:2, :] * pltpu.roll(xe, 1, 0)[CONV_HALO:CONV_HALO + t]
    y = y + w[3:4, :] * pltpu.roll(xe, rows - 1, 0)[CONV_HALO:CONV_HALO + t]
    o_ref[...] = (_silu(y) if silu else y).astype(o_ref.dtype)


CONV_HALO = 16


def _dwconv(proj3, col0, width, w, bias, silu):
    b, s, _ = proj3.shape
    t = ROW_TILE
    tc = 1024
    cb = col0 // tc
    tps = s // t
    hb = t // CONV_HALO
    last = s // CONV_HALO - 1
    return pl.pallas_call(
        functools.partial(_conv_kernel, silu=silu, tps=tps),
        grid=(b, tps, width // tc),
        in_specs=[pl.BlockSpec((None, t, tc), lambda bi, j, c: (bi, j, cb + c)),
                  pl.BlockSpec((None, CONV_HALO, tc), lambda bi, j, c: (bi, jnp.maximum(j * hb - 1, 0), cb + c)),
                  pl.BlockSpec((None, CONV_HALO, tc), lambda bi, j, c: (bi, jnp.minimum((j + 1) * hb, last), cb + c)),
                  pl.BlockSpec((CONV_W, tc), lambda bi, j, c: (0, c)),
                  pl.BlockSpec((1, tc), lambda bi, j, c: (0, c))],
        out_specs=pl.BlockSpec((None, t, tc), lambda bi, j, c: (bi, j, c)),
        out_shape=jax.ShapeDtypeStruct((b, s, width), BF16),
        compiler_params=_cparams(),
        name="dwconv",
    )(proj3, proj3, proj3, w, bias)


def _ssd_kernel(xs0, b0, c0, dt0, xs1, b1, c1, dt1, dtb_ref, aneg_ref, e0_ref, e1_ref, y0_ref, y1_ref,
                h0_ref, h1_ref):
    @pl.when(pl.program_id(1) == 0)
    def _():
        h0_ref[...] = jnp.zeros_like(h0_ref)
        h1_ref[...] = jnp.zeros_like(h1_ref)

    _ssd_chunk(xs0, b0, c0, dt0, dtb_ref, aneg_ref, e0_ref, y0_ref, h0_ref, rev=False)
    _ssd_chunk(xs1, b1, c1, dt1, dtb_ref, aneg_ref, e1_ref, y1_ref, h1_ref, rev=True)


def _ssd_chunk(xs_ref, b_ref, c_ref, dt_ref, dtb_ref, aneg_ref, e_ref, y_ref, h_ref, *, rev):
    q = SSD_CHUNK
    dt = _softplus(dt_ref[...] + dtb_ref[...])
    a = dt * aneg_ref[...]
    row = lax.broadcasted_iota(jnp.int32, (q, q), 0)
    col = lax.broadcasted_iota(jnp.int32, (q, q), 1)
    tri = (col >= row) if rev else (col <= row)
    tri_b = jnp.where(tri, 1.0, 0.0).astype(BF16)
    a3 = _split3(a)
    acum = _dot(tri_b, a3[0]) + _dot(tri_b, a3[1]) + _dot(tri_b, a3[2])
    total = jnp.sum(a, axis=0, keepdims=True)
    stack = jnp.concatenate([dt, total - acum, acum, jnp.broadcast_to(total, (SUBLANES, LANES))], axis=0)
    e = e_ref[...]
    s3 = _split3(stack)
    ex = _dot(s3[0], e) + _dot(s3[1], e) + _dot(s3[2], e)
    dt_e = ex[0:q]
    to_end_e = jnp.exp(ex[q:2 * q])
    from_start_e = jnp.exp(ex[2 * q:3 * q])
    chunk_decay_e = jnp.exp(ex[3 * q:3 * q + 1])
    xg = xs_ref[...].astype(F32) * dt_e
    xg_b = xg.astype(BF16)
    w_b = (xg * to_end_e).astype(BF16)
    acum_t = acum.T
    base = DT_DIR_STRIDE if rev else 0
    for g in range(SSD_GROUPS):
        gs = slice(g * GROUP_W, (g + 1) * GROUP_W)
        bg = b_ref[:, g * SSD_STATE:(g + 1) * SSD_STATE]
        cg = c_ref[:, g * SSD_STATE:(g + 1) * SSD_STATE]
        cb = _dot_nt(cg, bg)
        h_prev = h_ref[:, gs]
        y_off = _dot(cg, h_prev.astype(BF16)) * from_start_e[:, gs]
        h_ref[:, gs] = chunk_decay_e[:, gs] * h_prev + _dot(bg.astype(F32).T.astype(BF16), w_b[:, gs])
        parts = []
        for hh in range(SSD_HEADS // SSD_GROUPS):
            head = g * (SSD_HEADS // SSD_GROUPS) + hh
            c = base + head
            seg = acum[:, c:c + 1] - acum_t[c:c + 1, :]
            lmat = jnp.exp(jnp.where(tri, seg, -1e30))
            parts.append(_dot((cb * lmat).astype(BF16), xg_b[:, head * SSD_HEADDIM:(head + 1) * SSD_HEADDIM]))
        y_ref[:, gs] = (jnp.concatenate(parts, axis=1) + y_off).astype(y_ref.dtype)


def _ssd_order(step, n_ctx_chunks, n_chunks, rev):
    if not rev:
        return step
    return jnp.where(step < n_ctx_chunks, n_ctx_chunks - 1 - step, n_chunks - 1 + n_ctx_chunks - step)


def _ssd_scan(xbc_conv, misc3, dt_bias_row, a_neg_row, expand, n_ctx):
    b, s, _ = xbc_conv.shape
    q = SSD_CHUNK
    nchunks = s // q
    const = lambda shape: pl.BlockSpec(shape, lambda bi, i: (0, 0))
    specs, out_specs = [], []
    for rev in (False, True):
        order = functools.partial(_ssd_order, n_ctx_chunks=n_ctx // q, n_chunks=nchunks, rev=rev)
        chunk = lambda w, cb, order=order: pl.BlockSpec((None, q, w), lambda bi, i: (bi, order(i), cb))
        specs += [chunk(SSD_INNER, 0), chunk(SSD_GROUPS * SSD_STATE, 2), chunk(SSD_GROUPS * SSD_STATE, 3),
                  chunk(LANES, (COL_DT - COL_KR) // LANES)]
        out_specs.append(chunk(SSD_INNER, 0))
    return pl.pallas_call(
        _ssd_kernel,
        grid=(b, nchunks),
        in_specs=specs + [const((1, LANES)), const((1, LANES)), const((LANES, SSD_INNER)), const((LANES, SSD_INNER))],
        out_specs=out_specs,
        out_shape=[jax.ShapeDtypeStruct((b, s, SSD_INNER), BF16)] * 2,
        scratch_shapes=[pltpu.VMEM((SSD_STATE, SSD_INNER), F32)] * 2,
        compiler_params=_cparams(),
        name="ssd_scan",
    )(xbc_conv, xbc_conv, xbc_conv, misc3, xbc_conv, xbc_conv, xbc_conv, misc3,
      dt_bias_row, a_neg_row, expand[0], expand[1])


def _lru_kernel(x_ref, w_ref, ba_ref, bx_ref, lam_ref, o_ref, a_s, b_s, h_s, carry, *, rev):
    nb, t, _ = x_ref.shape
    per = LRU_GROUP // LANES

    @pl.when(pl.program_id(0) == 0)
    def _():
        carry[...] = jnp.zeros_like(carry)

    decay = -LRU_C * _softplus(-lam_ref[...])
    for bi in range(nb):
        for g in range(LRU_WIDTH // LRU_GROUP):
            gs = slice(g * LRU_GROUP, (g + 1) * LRU_GROUP)
            xd = x_ref[bi, :, gs]
            ri = _dot(xd, w_ref[g])
            r = _sigmoid_tanh(ri[:, :LRU_GROUP] + ba_ref[:, gs])
            i = _sigmoid_tanh(ri[:, LRU_GROUP:] + bx_ref[:, gs])
            log_a = decay[:, gs] * r
            a = jnp.exp(log_a)
            bt = jnp.sqrt(jnp.tanh(-log_a) * (1.0 + a * a)) * (i * xd.astype(F32))
            for cc in range(per):
                ls = slice(cc * LANES, (cc + 1) * LANES)
                a_s[g * per + cc, pl.ds(bi, t, stride=nb), :] = a[:, ls]
                b_s[g * per + cc, pl.ds(bi, t, stride=nb), :] = bt[:, ls]

    def step(k, h):
        tt = (t - 1 - k) if rev else k
        r0 = pl.multiple_of(tt * nb, nb)
        h = a_s[:, pl.ds(r0, nb), :] * h + b_s[:, pl.ds(r0, nb), :]
        h_s[:, pl.ds(r0, nb), :] = h
        return h

    carry[...] = lax.fori_loop(0, t, step, carry[...])
    for bi in range(nb):
        for cc in range(LRU_WIDTH // LANES):
            o_ref[bi, :, cc * LANES:(cc + 1) * LANES] = h_s[cc, pl.ds(bi, t, stride=nb), :].astype(o_ref.dtype)


def _lru_scan(lx_conv, w_gate, ba, bx, lam, n_ctx, rev):
    b, s, width = lx_conv.shape
    t = LRU_TILE
    ntiles = s // t
    order = functools.partial(_ssd_order, n_ctx_chunks=n_ctx // t, n_chunks=ntiles, rev=rev)
    const = lambda shape: pl.BlockSpec(shape, lambda i: (0,) * len(shape))
    return pl.pallas_call(
        functools.partial(_lru_kernel, rev=rev),
        grid=(ntiles,),
        in_specs=[pl.BlockSpec((b, t, width), lambda i: (0, order(i), 0)),
                  const(w_gate.shape), const((1, width)), const((1, width)), const((1, width))],
        out_specs=pl.BlockSpec((b, t, width), lambda i: (0, order(i), 0)),
        out_shape=jax.ShapeDtypeStruct((b, s, width), BF16),
        scratch_shapes=[pltpu.VMEM((width // LANES, t * b, LANES), F32)] * 3
        + [pltpu.VMEM((width // LANES, b, LANES), F32)],
        compiler_params=_cparams(),
        name="lru_rev" if rev else "lru_fwd",
    )(lx_conv, w_gate, ba, bx, lam)


def _gelu_tanh(x):
    return 0.5 * x * (1.0 + jnp.tanh(math.sqrt(2.0 / math.pi) * (x + 0.044715 * (x * x * x))))


def _merge_kernel(att_ref, y0_ref, y1_ref, xs_ref, z_ref, l0_ref, l1_ref, lg_ref, gate_ref, h_ref,
                  dskip_ref, ssdg_ref, wb_ref, wo_ref, g1_ref, lng_ref, lnb_ref, sh2_ref, sc2_ref,
                  h1_ref, u2_ref, *, alpha):
    f32 = lambda ref: ref[...].astype(F32)
    y = f32(y0_ref) + f32(y1_ref) + f32(xs_ref) * dskip_ref[...]
    y = y * _silu(f32(z_ref))
    parts = []
    for g in range(SSD_GROUPS):
        yg = y[:, g * GROUP_W:(g + 1) * GROUP_W]
        parts.append(yg * lax.rsqrt(jnp.mean(yg * yg, axis=-1, keepdims=True) + RMS_EPS))
    y_ssd = jnp.concatenate(parts, axis=1) * ssdg_ref[...]
    y_lru = (f32(l0_ref) + f32(l1_ref)) * _gelu_tanh(f32(lg_ref))
    branches = (att_ref[...], y_ssd.astype(BF16), y_lru.astype(BF16))
    mix = None
    for k in range(N_BRANCH):
        gate = gate_ref[:, k * D_MODEL:(k + 1) * D_MODEL].astype(F32)
        term = _sigmoid(gate) * _dot(branches[k], wb_ref[k])
        mix = term if mix is None else mix + term
    out = _dot(mix.astype(BF16), wo_ref[...])
    h1 = _layer_norm(alpha * h_ref[...] + g1_ref[...] * out) * lng_ref[...] + lnb_ref[...]
    h1_ref[...] = h1
    u2 = _layer_norm(h1) * (1.0 + sc2_ref[...]) + sh2_ref[...]
    for s in range(D_MODEL // LANES):
        u2_ref[:, s, :] = u2[:, s * LANES:(s + 1) * LANES]


def _merge(att, y0, y1, xbc_conv, proj, l0, l1, h, d_skip_row, ssd_gain, wb_all, wo_all, layer, mods, ln_g, ln_b,
           tps, alpha):
    n = h.shape[0]
    t = ROW_TILE
    d = D_MODEL
    row = lambda cb: pl.BlockSpec((t, d), lambda i: (i, cb))
    const = lambda shape: pl.BlockSpec(shape, lambda i: (0,) * len(shape))
    stacked = lambda shape: pl.BlockSpec((None,) + shape, lambda i: (layer,) + (0,) * len(shape))
    return pl.pallas_call(
        functools.partial(_merge_kernel, alpha=alpha),
        grid=(n // t,),
        in_specs=[row(0), row(0), row(0), row(0), row(COL_Z // d), row(0), row(0), row(COL_LG // d),
                  pl.BlockSpec((t, N_BRANCH * d), lambda i: (i, COL_GATE // (N_BRANCH * d))), row(0),
                  const((1, d)), const((1, d)), stacked(wb_all.shape[1:]), stacked(wo_all.shape[1:]),
                  _mod_spec(tps, 2), const((1, d)), const((1, d)), _mod_spec(tps, 3), _mod_spec(tps, 4)],
        out_specs=[row(0), pl.BlockSpec((t, d // LANES, LANES), lambda i: (i, 0, 0))],
        out_shape=[jax.ShapeDtypeStruct((n, d), F32),
                   jax.ShapeDtypeStruct((n, d // LANES, LANES), F32)],
        compiler_params=_cparams(),
        name="merge",
    )(att, y0, y1, xbc_conv, proj, l0, l1, proj, proj, h, d_skip_row, ssd_gain, wb_all, wo_all,
      mods, ln_g, ln_b, mods, mods)


def _rows_from_tiles(ref):
    return jnp.concatenate([ref[:, s, :] for s in range(ref.shape[1])], axis=1)


def _router_kernel(u_ref, w_ref, b_ref, meta_ref, cnt_ref, count):
    @pl.when(pl.program_id(0) == 0)
    def _():
        count[...] = jnp.zeros_like(count)

    u = _rows_from_tiles(u_ref)
    t = u.shape[0]
    uh = u.astype(BF16)
    ul = (u - uh.astype(F32)).astype(BF16)
    w = w_ref[...]
    wh = w.astype(BF16)
    wl = (w - wh.astype(F32)).astype(BF16)
    logits = _dot(uh, wh) + _dot(uh, wl) + _dot(ul, wh) + b_ref[...]
    lane = lax.broadcasted_iota(jnp.int32, logits.shape, 1)
    neg = -jnp.inf
    big = 4 * LANES

    def top1(vals):
        m = jnp.max(vals, axis=-1, keepdims=True)
        idx = jnp.min(jnp.where(vals == m, lane, big), axis=-1, keepdims=True)
        return m, idx

    glog = jnp.where(lane < N_GROUPS, logits, neg)
    gmax, gsel = top1(glog)
    gval = 1.0 / jnp.sum(jnp.exp(glog - gmax), axis=-1, keepdims=True)
    lo = N_GROUPS + gsel * EXPERTS_PER_GROUP
    elog = jnp.where(jnp.logical_and(lane >= lo, lane < lo + EXPERTS_PER_GROUP), logits, neg)
    v1, i1 = top1(elog)
    v2, i2 = top1(jnp.where(lane == i1, neg, elog))
    e21 = jnp.exp(v2 - v1)
    w1 = gval / (1.0 + e21)
    w2 = gval * e21 / (1.0 + e21)
    e1 = i1 - N_GROUPS
    e2 = i2 - N_GROUPS
    onehot = jnp.logical_or(lane == e1, lane == e2)
    oh = jnp.where(onehot, 1.0, 0.0)
    r = lax.broadcasted_iota(jnp.int32, (t, t), 0)
    c = lax.broadcasted_iota(jnp.int32, (t, t), 1)
    before = jnp.where(c < r, 1.0, 0.0).astype(BF16)
    prefix = _dot(before, oh.astype(BF16)) + count[0:1, :]
    rank1 = jnp.sum(jnp.where(lane == e1, prefix, 0.0), axis=-1, keepdims=True)
    rank2 = jnp.sum(jnp.where(lane == e2, prefix, 0.0), axis=-1, keepdims=True)
    meta = jnp.where(lane == 0, e1.astype(F32), 0.0)
    meta = jnp.where(lane == 1, e2.astype(F32), meta)
    meta = jnp.where(lane == 2, w1, meta)
    meta = jnp.where(lane == 3, w2, meta)
    meta = jnp.where(lane == 4, rank1, meta)
    meta = jnp.where(lane == 5, rank2, meta)
    meta_ref[...] = meta
    count[...] = count[...] + jnp.sum(oh, axis=0, keepdims=True)
    cnt_ref[...] = count[...]


def _router(u2t, w_router, b_router):
    n = u2t.shape[0]
    t = ROW_TILE
    return pl.pallas_call(
        _router_kernel,
        grid=(n // t,),
        in_specs=[pl.BlockSpec((t, D_MODEL // LANES, LANES), lambda i: (i, 0, 0)),
                  pl.BlockSpec((D_MODEL, LANES), lambda i: (0, 0)),
                  pl.BlockSpec((1, LANES), lambda i: (0, 0))],
        out_specs=[pl.BlockSpec((t, LANES), lambda i: (i, 0)),
                   pl.BlockSpec((SUBLANES, LANES), lambda i: (0, 0))],
        out_shape=[jax.ShapeDtypeStruct((n, LANES), F32),
                   jax.ShapeDtypeStruct((SUBLANES, LANES), F32)],
        scratch_shapes=[pltpu.VMEM((SUBLANES, LANES), F32)],
        compiler_params=_cparams(),
        name="router",
    )(u2t, w_router, b_router)


DISPATCH_TILE = 512


def _dispatch_kernel(dest_ref, u_ref, init_ref, x_ref, sem):
    del init_ref
    base = pl.program_id(0) * DISPATCH_TILE

    def copy(k, slot):
        return pltpu.make_async_copy(u_ref.at[k], x_ref.at[dest_ref[TOP_K * (base + k) + slot]], sem)

    def issue(k, carry):
        for slot in range(TOP_K):
            copy(k, slot).start()
        return carry

    def drain(k, carry):
        for slot in range(TOP_K):
            copy(k, slot).wait()
        return carry

    lax.fori_loop(0, DISPATCH_TILE, issue, 0)
    lax.fori_loop(0, DISPATCH_TILE, drain, 0)


def _dispatch(dest_flat, u2t, n_rows):
    n = u2t.shape[0]
    init = jnp.zeros((n_rows,) + u2t.shape[1:], u2t.dtype)
    return pl.pallas_call(
        _dispatch_kernel,
        grid_spec=pltpu.PrefetchScalarGridSpec(
            num_scalar_prefetch=1,
            grid=(n // DISPATCH_TILE,),
            in_specs=[pl.BlockSpec((DISPATCH_TILE,) + u2t.shape[1:], lambda i, dest: (i, 0, 0)),
                      pl.BlockSpec(memory_space=pl.ANY)],
            out_specs=pl.BlockSpec(memory_space=pl.ANY),
            scratch_shapes=[pltpu.SemaphoreType.DMA(())]),
        out_shape=jax.ShapeDtypeStruct(init.shape, init.dtype),
        input_output_aliases={2: 0},
        compiler_params=_cparams(has_side_effects=True),
        name="moe_dispatch",
    )(dest_flat, u2t, init)


def _expert_kernel(blk_ref, nact_ref, x_ref, w1_ref, w3_ref, w2_ref, y_ref, w1_b, w3_b, w2_b):
    i = pl.program_id(0)

    @pl.when(jnp.logical_or(i == 0, blk_ref[i] != blk_ref[jnp.maximum(i - 1, 0)]))
    def _():
        w1_b[...] = w1_ref[...].astype(BF16)
        w3_b[...] = w3_ref[...].astype(BF16)
        w2_b[...] = w2_ref[...].astype(BF16)

    @pl.when(i < nact_ref[0])
    def _():
        x = _rows_from_tiles(x_ref).astype(BF16)
        hid = _silu(_dot(x, w1_b[...])) * _dot(x, w3_b[...])
        y = _dot(hid.astype(BF16), w2_b[...])
        for s in range(D_MODEL // LANES):
            y_ref[:, s, :] = y[:, s * LANES:(s + 1) * LANES]

    @pl.when(i >= nact_ref[0])
    def _():
        y_ref[...] = jnp.zeros_like(y_ref)


def _experts(blk_e, nact, xin, w1_all, w3_all, w2_all, layer):
    nb = xin.shape[0] // EXPERT_BLOCK
    tile = (EXPERT_BLOCK, D_MODEL // LANES, LANES)
    weight = lambda r, c: pl.BlockSpec((None, None, r, c), lambda i, blk, na: (layer, blk[i], 0, 0))
    return pl.pallas_call(
        _expert_kernel,
        grid_spec=pltpu.PrefetchScalarGridSpec(
            num_scalar_prefetch=2,
            grid=(nb,),
            in_specs=[pl.BlockSpec(tile, lambda i, blk, na: (i, 0, 0)),
                      weight(D_MODEL, EXPERT_HIDDEN), weight(D_MODEL, EXPERT_HIDDEN), weight(EXPERT_HIDDEN, D_MODEL)],
            out_specs=pl.BlockSpec(tile, lambda i, blk, na: (i, 0, 0)),
            scratch_shapes=[pltpu.VMEM((D_MODEL, EXPERT_HIDDEN), BF16), pltpu.VMEM((D_MODEL, EXPERT_HIDDEN), BF16),
                            pltpu.VMEM((EXPERT_HIDDEN, D_MODEL), BF16)]),
        out_shape=jax.ShapeDtypeStruct(xin.shape, F32),
        compiler_params=_cparams(),
        name="moe_experts",
    )(blk_e, nact, xin, w1_all, w3_all, w2_all)


COMBINE_TILE = 256


def _combine_kernel(dest_ref, y_ref, meta_ref, h_ref, g2_ref, lng_ref, lnb_ref, o_ref, buf, sem, *, alpha):
    t = COMBINE_TILE
    base = pl.program_id(0) * t

    def copy(k, slot):
        return pltpu.make_async_copy(y_ref.at[dest_ref[TOP_K * (base + k) + slot]], buf.at[slot, k], sem)

    def issue(k, carry):
        for slot in range(TOP_K):
            copy(k, slot).start()
        return carry

    def drain(k, carry):
        for slot in range(TOP_K):
            copy(k, slot).wait()
        return carry

    lax.fori_loop(0, t, issue, 0)
    lax.fori_loop(0, t, drain, 0)
    meta = meta_ref[...]
    f = meta[:, 2:3] * _rows_from_tiles(buf.at[0]) + meta[:, 3:4] * _rows_from_tiles(buf.at[1])
    o_ref[...] = _layer_norm(alpha * h_ref[...] + g2_ref[...] * f) * lng_ref[...] + lnb_ref[...]


def _combine(dest_flat, yout, meta, h1, mods, ln_g, ln_b, tps, alpha):
    n = h1.shape[0]
    t = COMBINE_TILE
    d = D_MODEL

    def mod_map(i, dest):
        return (jnp.where(i % tps == 0, 8, i // tps), 0, 5)

    return pl.pallas_call(
        functools.partial(_combine_kernel, alpha=alpha),
        grid_spec=pltpu.PrefetchScalarGridSpec(
            num_scalar_prefetch=1,
            grid=(n // t,),
            in_specs=[pl.BlockSpec(memory_space=pl.ANY),
                      pl.BlockSpec((t, LANES), lambda i, dest: (i, 0)),
                      pl.BlockSpec((t, d), lambda i, dest: (i, 0)),
                      pl.BlockSpec((None, 1, d), mod_map),
                      pl.BlockSpec((1, d), lambda i, dest: (0, 0)),
                      pl.BlockSpec((1, d), lambda i, dest: (0, 0))],
            out_specs=pl.BlockSpec((t, d), lambda i, dest: (i, 0)),
            scratch_shapes=[pltpu.VMEM((TOP_K, t, d // LANES, LANES), F32), pltpu.SemaphoreType.DMA(())]),
        out_shape=jax.ShapeDtypeStruct((n, d), F32),
        compiler_params=_cparams(),
        name="moe_combine",
    )(dest_flat, yout, meta, h1, mods, ln_g, ln_b)


def _prep_w_in(w_in):
    cq, ckv, kr, z, xbc, dt, lx, lg, gate = _split_sections(w_in)
    zeros = lambda w: jnp.zeros(w_in.shape[:-1] + (w,), w_in.dtype)
    out = jnp.concatenate([z, lx, lg, gate, xbc, cq, ckv, kr, zeros(64),
                           dt[..., :SSD_HEADS], zeros(DT_DIR_STRIDE - SSD_HEADS),
                           dt[..., SSD_HEADS:], zeros(DT_DIR_STRIDE - SSD_HEADS)], axis=-1)
    return out.astype(BF16)


def _split_sections(w):
    sizes = (Q_LORA, KV_LORA, QK_ROPE, SSD_INNER, SSD_INNER + 2 * SSD_GROUPS * SSD_STATE, 2 * SSD_HEADS,
             LRU_WIDTH, LRU_WIDTH, N_BRANCH * D_MODEL)
    out, start = [], 0
    for size in sizes:
        out.append(w[..., start:start + size])
        start += size
    return out


def _dir_row(v):
    row = jnp.zeros((LANES,), F32)
    row = row.at[0:SSD_HEADS].set(v[0]).at[DT_DIR_STRIDE:DT_DIR_STRIDE + SSD_HEADS].set(v[1])
    return row[None, :]


def _expand_matrix(rev):
    rows = jnp.arange(LANES)[:, None]
    cols = jnp.arange(SSD_INNER)[None, :]
    base = DT_DIR_STRIDE if rev else 0
    return (rows - base == cols // SSD_HEADDIM).astype(BF16)


def _block_diag_gates(wa, wx):
    per = LRU_GROUP // LRU_BW
    eye = jnp.eye(per, dtype=wa.dtype)

    def bd(w):
        w = w.reshape(LRU_WIDTH // LRU_GROUP, per, LRU_BW, LRU_BW)
        return jnp.einsum('gicd,ij->gicjd', w, eye).reshape(LRU_WIDTH // LRU_GROUP, LRU_GROUP, LRU_GROUP)

    return jnp.concatenate([bd(wa), bd(wx)], axis=-1).astype(BF16)


def _rope_tables(n_ctx, seq):
    rows = seq // GRID_W
    row_pos = jnp.repeat(jnp.arange(rows, dtype=F32), GRID_W)
    col_pos = (jnp.arange(rows * GRID_W) % GRID_W).astype(F32)
    inv_freq = ROPE_THETA ** (-jnp.arange(ROPE_FREQS, dtype=F32) / ROPE_FREQS)
    ang = [row_pos[:, None] * inv_freq, col_pos[:, None] * inv_freq]
    cos = jnp.concatenate([jnp.cos(ang[0]), jnp.cos(ang[0]), jnp.cos(ang[1]), jnp.cos(ang[1])], axis=1)
    sin = jnp.concatenate([-jnp.sin(ang[0]), jnp.sin(ang[0]), -jnp.sin(ang[1]), jnp.sin(ang[1])], axis=1)
    pad = lambda t, fill: jnp.concatenate(
        [jnp.concatenate([jnp.full((n_ctx, QK_ROPE), fill, F32), t], axis=0),
         jnp.zeros((n_ctx + seq, LANES - QK_ROPE), F32)], axis=1)
    return pad(cos, 1.0), pad(sin, 0.0)


def kernel(x, c, ctx, c_ctx, w_mod, b_mod, w_in, q_norm_w, kv_norm_w, w_uq, w_ukv, ssd_conv_w, ssd_conv_b, ssd_a_log, ssd_dt_bias, ssd_d, ssd_norm_w, lru_conv_w, lru_conv_b, lru_wa, lru_ba, lru_wx, lru_bx, lru_lambda, w_branch, w_out, ln1_g, ln1_b, ln2_g, ln2_b, router_wg, router_bg, router_we, router_be, exp_w1, exp_w3, exp_w2):
    bsz, seq, d = x.shape
    n_ctx = ctx.shape[1]
    depth = w_mod.shape[0]
    assert d == D_MODEL and n_ctx == ROW_TILE and seq % ROW_TILE == 0 and bsz == SUBLANES
    s_all = n_ctx + seq
    tps = s_all // ROW_TILE
    n = bsz * s_all
    alpha = (2 * depth) ** 0.25

    h = jnp.concatenate([ctx, x], axis=1).reshape(n, d)
    cvec = jnp.zeros((16, d), F32).at[:bsz].set(c).at[bsz].set(c_ctx)
    mods_all = _mod_vectors(cvec, w_mod, b_mod).reshape(depth, 16, 1, 6 * d)
    cos_t, sin_t = _rope_tables(n_ctx, seq)
    expand = (_expand_matrix(False), _expand_matrix(True))

    w_in_p = _prep_w_in(w_in)
    wq = jnp.pad(w_uq.reshape(depth, Q_LORA, MLA_HEADS, QK_NOPE + QK_ROPE),
                 ((0, 0), (0, 0), (0, 0), (0, QK_PAD - QK_NOPE - QK_ROPE))).reshape(depth, Q_LORA, -1).astype(BF16)
    wkv4 = w_ukv.reshape(depth, KV_LORA, MLA_HEADS, QK_NOPE + V_DIM)
    wkv = jnp.concatenate([wkv4[..., :QK_NOPE].reshape(depth, KV_LORA, -1),
                           wkv4[..., QK_NOPE:].reshape(depth, KV_LORA, -1)], axis=-1).astype(BF16)
    w_router = jnp.concatenate([router_wg, router_we,
                                jnp.zeros((depth, d, LANES - N_GROUPS - N_EXPERTS), F32)], axis=-1)
    b_router = jnp.concatenate([router_bg, router_be,
                                jnp.zeros((depth, LANES - N_GROUPS - N_EXPERTS), F32)], axis=-1)
    wb_all = w_branch.astype(BF16)
    wo_all = w_out.astype(BF16)

    n_blocks = -(-(n * TOP_K) // EXPERT_BLOCK) + N_EXPERTS
    row1 = lambda v: v.reshape(1, -1)

    for l in range(depth):
        mods = mods_all[l]
        proj, misc = _in_proj(h, mods, w_in_p, l, tps)
        proj3 = proj.reshape(bsz, s_all, -1)
        misc3 = misc.reshape(bsz, s_all, -1)

        q, k, vt = _mla_prep(proj, misc, row1(q_norm_w[l]), row1(kv_norm_w[l]), wq, wkv, l, cos_t, sin_t, tps)
        att = _attention(q.reshape(bsz, s_all, -1), k.reshape(bsz, s_all, -1), vt, n_ctx)

        xbc_conv = _dwconv(proj3, COL_XBC, 2 * SSD_INNER, ssd_conv_w[l], row1(ssd_conv_b[l]), True)
        dtb = _dir_row(ssd_dt_bias[l])
        aneg = _dir_row(-jnp.exp(ssd_a_log[l]))
        ys = _ssd_scan(xbc_conv, misc3, dtb, aneg, expand, n_ctx)

        lx_conv = _dwconv(proj3, COL_LX, LRU_WIDTH, lru_conv_w[l], row1(lru_conv_b[l]), False)
        hs = [_lru_scan(lx_conv, _block_diag_gates(lru_wa[l, dr], lru_wx[l, dr]), row1(lru_ba[l, dr]),
                        row1(lru_bx[l, dr]), row1(lru_lambda[l, dr]), n_ctx, bool(dr)) for dr in range(2)]

        d_skip_row = row1(jnp.repeat(ssd_d[l], SSD_HEADDIM))
        h1, u2t = _merge(att.reshape(n, -1), ys[0].reshape(n, -1), ys[1].reshape(n, -1),
                         xbc_conv.reshape(n, -1), proj, hs[0].reshape(n, -1), hs[1].reshape(n, -1), h,
                         d_skip_row, row1(ssd_norm_w[l]), wb_all, wo_all, l, mods,
                         row1(ln1_g[l]), row1(ln1_b[l]), tps, alpha)

        meta, cnt = _router(u2t, w_router[l], row1(b_router[l]))
        counts = cnt[0, :N_EXPERTS].astype(jnp.int32)
        padded = (counts + EXPERT_BLOCK - 1) // EXPERT_BLOCK * EXPERT_BLOCK
        pend = jnp.cumsum(padded)
        pstart = pend - padded
        eid = meta[:, 0:TOP_K].astype(jnp.int32)
        rank = meta[:, 4:4 + TOP_K].astype(jnp.int32)
        dest = (jnp.take(pstart, eid) + rank).reshape(-1)
        blk_start = jnp.arange(n_blocks, dtype=jnp.int32) * EXPERT_BLOCK
        blk_e = jnp.minimum(jnp.sum(pend[None, :] <= blk_start[:, None], axis=1), N_EXPERTS - 1).astype(jnp.int32)
        nact = (pend[-1:] // EXPERT_BLOCK).astype(jnp.int32)
        blk_e = jnp.where(jnp.arange(n_blocks) < nact[0], blk_e, jnp.take(blk_e, jnp.maximum(nact[0] - 1, 0)))

        xin = _dispatch(dest, u2t, n_blocks * EXPERT_BLOCK)
        yout = _experts(blk_e, nact, xin, exp_w1, exp_w3, exp_w2, l)
        h = _combine(dest, yout, meta, h1, mods, row1(ln2_g[l]), row1(ln2_b[l]), tps, alpha)

    return h.reshape(bsz, s_all, d)[:, n_ctx:, :]
```

```python
import functools
import math

import jax
import jax.numpy as jnp
from jax import lax
from jax.experimental import pallas as pl
from jax.experimental.pallas import tpu as pltpu

F32 = jnp.float32
BF16 = jnp.bfloat16

D_MODEL = 1024
GRID_W = 64
N_BRANCH = 3

MLA_HEADS = 8
QK_NOPE = 128
QK_ROPE = 64
V_DIM = 128
Q_LORA = 512
KV_LORA = 256
ROPE_FREQS = QK_ROPE // 4
ROPE_THETA = 10000.0
ATTN_SCALE = (QK_NOPE + QK_ROPE) ** -0.5
QK_PAD = 256

SSD_HEADDIM = 64
SSD_INNER = 1024
SSD_HEADS = 16
SSD_GROUPS = 4
SSD_STATE = 128
SSD_CHUNK = 128
GROUP_W = SSD_INNER // SSD_GROUPS

CONV_W = 4
LRU_WIDTH = 1024
LRU_BW = 64
LRU_C = 8.0
LRU_TILE = 128
LRU_GROUP = 256

N_GROUPS = 4
EXPERTS_PER_GROUP = 8
N_EXPERTS = 32
TOP_K = 2
EXPERT_HIDDEN = 512
EXPERT_BLOCK = 256

LN_EPS = 1e-5
RMS_EPS = 1e-6

LANES = 128
SUBLANES = 8
ROW_TILE = 256
VMEM_LIMIT = 56 * 1024 * 1024

COL_Z = 0
COL_LX = 1024
COL_LG = 2048
COL_GATE = 3072
COL_XBC = 6144
COL_CQ = 8192
COL_CKV = 8704
COL_KR = 8960
COL_DT = 9088
DT_DIR_STRIDE = 64
IN_COLS_PAD = 9216


def _cparams(**kw):
    return pltpu.CompilerParams(vmem_limit_bytes=VMEM_LIMIT, **kw)


def _split3(x):
    hi = x.astype(BF16)
    r = x - hi.astype(F32)
    mid = r.astype(BF16)
    lo = (r - mid.astype(F32)).astype(BF16)
    return hi, mid, lo


def _dot(a, b):
    return jnp.dot(a, b, preferred_element_type=F32)


def _dot_nt(a, b):
    return lax.dot_general(a, b, (((1,), (1,)), ((), ())), preferred_element_type=F32)


def _sigmoid(x):
    return 1.0 / (1.0 + jnp.exp(-x))


def _sigmoid_tanh(x):
    return 0.5 * jnp.tanh(0.5 * x) + 0.5


def _silu(x):
    return x * _sigmoid(x)


def _softplus(x):
    return jnp.maximum(x, 0.0) + jnp.log(1.0 + jnp.exp(-jnp.abs(x)))


def _layer_norm(t):
    mu = jnp.mean(t, axis=-1, keepdims=True)
    c = t - mu
    var = jnp.mean(c * c, axis=-1, keepdims=True)
    return c * lax.rsqrt(var + LN_EPS)


def _mod_kernel(c_ref, w_ref, b_ref, o_ref):
    s = _silu(c_ref[...]).astype(BF16)
    o_ref[...] = _dot(s, w_ref[...].astype(BF16)) + b_ref[...]


def _mod_vectors(cvec, w_mod, b_mod):
    depth, d, n = w_mod.shape
    tn = 1536
    return pl.pallas_call(
        _mod_kernel,
        grid=(depth, n // tn),
        in_specs=[pl.BlockSpec((16, d), lambda l, j: (0, 0)),
                  pl.BlockSpec((None, d, tn), lambda l, j: (l, 0, j)),
                  pl.BlockSpec((None, 1, tn), lambda l, j: (l, 0, j))],
        out_specs=pl.BlockSpec((None, 16, tn), lambda l, j: (l, 0, j)),
        out_shape=jax.ShapeDtypeStruct((depth, 16, n), F32),
        compiler_params=_cparams(),
        name="mod_vectors",
    )(cvec, w_mod, b_mod.reshape(depth, 1, n))


def _mod_spec(tiles_per_sample, k):
    def imap(i):
        return (jnp.where(i % tiles_per_sample == 0, 8, i // tiles_per_sample), 0, k)
    return pl.BlockSpec((None, 1, D_MODEL), imap)


IN_PROJ_COLS = 1024


def _in_proj_kernel(h_ref, sh_ref, sc_ref, w_ref, o_ref, misc_ref):
    u = (_layer_norm(h_ref[...]) * (1.0 + sc_ref[...]) + sh_ref[...]).astype(BF16)
    main = o_ref.shape[1]
    for c0 in range(0, main, IN_PROJ_COLS):
        cs = slice(c0, min(c0 + IN_PROJ_COLS, main))
        o_ref[:, cs] = _dot(u, w_ref[:, cs]).astype(o_ref.dtype)
    misc_ref[...] = _dot(u, w_ref[:, main:])


def _in_proj(h, mods, w_all, layer, tps):
    n = h.shape[0]
    _, k, cols = w_all.shape
    return pl.pallas_call(
        _in_proj_kernel,
        grid=(n // ROW_TILE,),
        in_specs=[pl.BlockSpec((ROW_TILE, k), lambda i: (i, 0)),
                  _mod_spec(tps, 0), _mod_spec(tps, 1),
                  pl.BlockSpec((None, k, cols), lambda i: (layer, 0, 0), pipeline_mode=pl.Buffered(1))],
        out_specs=[pl.BlockSpec((ROW_TILE, COL_KR), lambda i: (i, 0)),
                   pl.BlockSpec((ROW_TILE, cols - COL_KR), lambda i: (i, 0))],
        out_shape=[jax.ShapeDtypeStruct((n, COL_KR), BF16),
                   jax.ShapeDtypeStruct((n, cols - COL_KR), F32)],
        compiler_params=_cparams(),
        name="in_proj",
    )(h, mods, mods, w_all)


def _rope128(t, cos, sin):
    lane = lax.broadcasted_iota(jnp.int32, t.shape, 1)
    partner = jnp.where(lane % (2 * ROPE_FREQS) < ROPE_FREQS,
                        pltpu.roll(t, LANES - ROPE_FREQS, 1), pltpu.roll(t, ROPE_FREQS, 1))
    return t * cos + partner * sin


def _rms(t, gain):
    return t * lax.rsqrt(jnp.mean(t * t, axis=-1, keepdims=True) + RMS_EPS) * gain


def _mla_prep_kernel(cq_ref, ckv_ref, kr_ref, qg_ref, kvg_ref, wq_ref, wkv_ref, cos_ref, sin_ref,
                     q_ref, k_ref, vt_ref):
    cos = cos_ref[...]
    sin = sin_ref[...]
    q = _dot(_rms(cq_ref[...].astype(F32), qg_ref[...]).astype(BF16), wq_ref[...]) * ATTN_SCALE
    kv = _dot(_rms(ckv_ref[...].astype(F32), kvg_ref[...]).astype(BF16), wkv_ref[...])
    krz = _rope128(kr_ref[...], cos, sin).astype(BF16)
    for h in range(MLA_HEADS):
        c0 = h * QK_PAD
        q_ref[:, c0:c0 + QK_NOPE] = q[:, c0:c0 + QK_NOPE].astype(BF16)
        q_ref[:, c0 + QK_NOPE:c0 + QK_PAD] = _rope128(q[:, c0 + QK_NOPE:c0 + QK_PAD], cos, sin).astype(BF16)
        k_ref[:, c0:c0 + QK_NOPE] = kv[:, h * QK_NOPE:(h + 1) * QK_NOPE].astype(BF16)
        k_ref[:, c0 + QK_NOPE:c0 + QK_PAD] = krz
    vt_ref[...] = kv[:, MLA_HEADS * QK_NOPE:].T.astype(BF16)


def _mla_prep(proj, misc, q_gain, kv_gain, wq_all, wkv_all, layer, cos_t, sin_t, tps):
    n = proj.shape[0]
    t = ROW_TILE
    row = lambda w, cb: pl.BlockSpec((t, w), lambda i: (i, cb))
    const = lambda shape: pl.BlockSpec(shape, lambda i: (0, 0))
    stacked = lambda shape: pl.BlockSpec((None,) + shape, lambda i: (layer, 0, 0))
    return pl.pallas_call(
        _mla_prep_kernel,
        grid=(n // t,),
        in_specs=[row(Q_LORA, COL_CQ // Q_LORA), row(KV_LORA, COL_CKV // KV_LORA), row(LANES, 0),
                  const((1, Q_LORA)), const((1, KV_LORA)),
                  stacked((Q_LORA, MLA_HEADS * QK_PAD)), stacked((KV_LORA, 2 * MLA_HEADS * QK_NOPE)),
                  pl.BlockSpec((t, LANES), lambda i: (i % tps, 0)),
                  pl.BlockSpec((t, LANES), lambda i: (i % tps, 0))],
        out_specs=[pl.BlockSpec((t, MLA_HEADS * QK_PAD), lambda i: (i, 0)),
                   pl.BlockSpec((t, MLA_HEADS * QK_PAD), lambda i: (i, 0)),
                   pl.BlockSpec((None, MLA_HEADS * V_DIM, t), lambda i: (i // tps, 0, i % tps))],
        out_shape=[jax.ShapeDtypeStruct((n, MLA_HEADS * QK_PAD), BF16),
                   jax.ShapeDtypeStruct((n, MLA_HEADS * QK_PAD), BF16),
                   jax.ShapeDtypeStruct((n // (tps * t), MLA_HEADS * V_DIM, tps * t), BF16)],
        compiler_params=_cparams(),
        name="mla_prep",
    )(proj, proj, misc, q_gain, kv_gain, wq_all, wkv_all, cos_t, sin_t)


ATTN_KEY_CHUNK = 768


def _attn_kernel(q_ref, k_ref, vt_ref, o_ref, s0_ref, s1_ref, *, n_ctx):
    s_all = k_ref.shape[0]
    tq = ROW_TILE
    n_tiles = (s_all - n_ctx) // tq
    chunks = [(c, min(c + ATTN_KEY_CHUNK, s_all)) for c in range(0, s_all, ATTN_KEY_CHUNK)]

    def scores(s_ref, r0, key_chunks):
        q = q_ref[pl.ds(r0, tq), :]
        for c0, c1 in key_chunks:
            s_ref[c0:c1, :] = _dot_nt(k_ref[c0:c1, :], q)

    def finish(s_ref, r0, key_chunks):
        m = None
        for c0, c1 in key_chunks:
            cm = jnp.max(s_ref[c0:c1, :], axis=0, keepdims=True)
            m = cm if m is None else jnp.maximum(m, cm)
        l = acc = None
        for c0, c1 in key_chunks:
            p = jnp.exp(s_ref[c0:c1, :] - m)
            ps = jnp.sum(p, axis=0, keepdims=True)
            pv = _dot(vt_ref[:, c0:c1], p.astype(BF16))
            l, acc = (ps, pv) if l is None else (l + ps, acc + pv)
        o_ref[pl.ds(r0, tq), :] = (acc / l).T.astype(o_ref.dtype)

    scores(s0_ref, 0, [(0, n_ctx)])
    finish(s0_ref, 0, [(0, n_ctx)])
    scores(s0_ref, n_ctx, chunks)

    def body(j, carry):
        r_a = pl.multiple_of(n_ctx + 2 * j * tq, tq)
        r_b = pl.multiple_of(r_a + tq, tq)
        r_c = pl.multiple_of(jnp.minimum(r_b + tq, s_all - tq), tq)
        scores(s1_ref, r_b, chunks)
        finish(s0_ref, r_a, chunks)
        scores(s0_ref, r_c, chunks)
        finish(s1_ref, r_b, chunks)
        return carry

    assert n_tiles % 2 == 0
    lax.fori_loop(0, n_tiles // 2, body, 0)


def _attention(q, k, vt, n_ctx):
    b, s, _ = q.shape
    return pl.pallas_call(
        functools.partial(_attn_kernel, n_ctx=n_ctx),
        grid=(b, MLA_HEADS),
        in_specs=[pl.BlockSpec((None, s, QK_PAD), lambda bi, h: (bi, 0, h)),
                  pl.BlockSpec((None, s, QK_PAD), lambda bi, h: (bi, 0, h)),
                  pl.BlockSpec((None, V_DIM, s), lambda bi, h: (bi, h, 0))],
        out_specs=pl.BlockSpec((None, s, V_DIM), lambda bi, h: (bi, 0, h)),
        out_shape=jax.ShapeDtypeStruct((b, s, MLA_HEADS * V_DIM), BF16),
        scratch_shapes=[pltpu.VMEM((s, ROW_TILE), F32), pltpu.VMEM((s, ROW_TILE), F32)],
        compiler_params=_cparams(),
        name="attention",
    )(q, k, vt)


def _conv_kernel(x_ref, p_ref, n_ref, w_ref, b_ref, o_ref, *, silu, tps):
    j = pl.program_id(1)
    has_prev = (j > 1).astype(F32)
    has_next = jnp.logical_and(j > 0, j < tps - 1).astype(F32)
    x = x_ref[...].astype(F32)
    xe = jnp.concatenate([p_ref[...].astype(F32) * has_prev, x, n_ref[...].astype(F32) * has_next], axis=0)
    rows = xe.shape[0]
    t = x_ref.shape[0]
    w = w_ref[...]
    y = b_ref[...] + w[2:3, :] * x
    y = y + w[0:1, :] * pltpu.roll(xe, 2, 0)[CONV_HALO:CONV_HALO + t]
    y = y + w[1:2, :] * pltpu.roll(xe, 1, 0)[CONV_HALO:CONV_HALO + t]
    y = y + w[3:4, :] * pltpu.roll(xe, rows - 1, 0)[CONV_HALO:CONV_HALO + t]
    o_ref[...] = (_silu(y) if silu else y).astype(o_ref.dtype)


CONV_HALO = 16


def _dwconv(proj3, col0, width, w, bias, silu):
    b, s, _ = proj3.shape
    t = ROW_TILE
    tc = 1024
    cb = col0 // tc
    tps = s // t
    hb = t // CONV_HALO
    last = s // CONV_HALO - 1
    return pl.pallas_call(
        functools.partial(_conv_kernel, silu=silu, tps=tps),
        grid=(b, tps, width // tc),
        in_specs=[pl.BlockSpec((None, t, tc), lambda bi, j, c: (bi, j, cb + c)),
                  pl.BlockSpec((None, CONV_HALO, tc), lambda bi, j, c: (bi, jnp.maximum(j * hb - 1, 0), cb + c)),
                  pl.BlockSpec((None, CONV_HALO, tc), lambda bi, j, c: (bi, jnp.minimum((j + 1) * hb, last), cb + c)),
                  pl.BlockSpec((CONV_W, tc), lambda bi, j, c: (0, c)),
                  pl.BlockSpec((1, tc), lambda bi, j, c: (0, c))],
        out_specs=pl.BlockSpec((None, t, tc), lambda bi, j, c: (bi, j, c)),
        out_shape=jax.ShapeDtypeStruct((b, s, width), BF16),
        compiler_params=_cparams(),
        name="dwconv",
    )(proj3, proj3, proj3, w, bias)


def _ssd_kernel(xs0, b0, c0, dt0, xs1, b1, c1, dt1, dtb_ref, aneg_ref, e0_ref, e1_ref, y0_ref, y1_ref,
                h0_ref, h1_ref):
    @pl.when(pl.program_id(1) == 0)
    def _():
        h0_ref[...] = jnp.zeros_like(h0_ref)
        h1_ref[...] = jnp.zeros_like(h1_ref)

    _ssd_chunk(xs0, b0, c0, dt0, dtb_ref, aneg_ref, e0_ref, y0_ref, h0_ref, rev=False)
    _ssd_chunk(xs1, b1, c1, dt1, dtb_ref, aneg_ref, e1_ref, y1_ref, h1_ref, rev=True)


def _ssd_chunk(xs_ref, b_ref, c_ref, dt_ref, dtb_ref, aneg_ref, e_ref, y_ref, h_ref, *, rev):
    q = SSD_CHUNK
    dt = _softplus(dt_ref[...] + dtb_ref[...])
    a = dt * aneg_ref[...]
    row = lax.broadcasted_iota(jnp.int32, (q, q), 0)
    col = lax.broadcasted_iota(jnp.int32, (q, q), 1)
    tri = (col >= row) if rev else (col <= row)
    tri_b = jnp.where(tri, 1.0, 0.0).astype(BF16)
    a3 = _split3(a)
    acum = _dot(tri_b, a3[0]) + _dot(tri_b, a3[1]) + _dot(tri_b, a3[2])
    total = jnp.sum(a, axis=0, keepdims=True)
    stack = jnp.concatenate([dt, total - acum, acum, jnp.broadcast_to(total, (SUBLANES, LANES))], axis=0)
    e = e_ref[...]
    s3 = _split3(stack)
    ex = _dot(s3[0], e) + _dot(s3[1], e) + _dot(s3[2], e)
    dt_e = ex[0:q]
    to_end_e = jnp.exp(ex[q:2 * q])
    from_start_e = jnp.exp(ex[2 * q:3 * q])
    chunk_decay_e = jnp.exp(ex[3 * q:3 * q + 1])
    xg = xs_ref[...].astype(F32) * dt_e
    xg_b = xg.astype(BF16)
    w_b = (xg * to_end_e).astype(BF16)
    acum_t = acum.T
    base = DT_DIR_STRIDE if rev else 0
    for g in range(SSD_GROUPS):
        gs = slice(g * GROUP_W, (g + 1) * GROUP_W)
        bg = b_ref[:, g * SSD_STATE:(g + 1) * SSD_STATE]
        cg = c_ref[:, g * SSD_STATE:(g + 1) * SSD_STATE]
        cb = _dot_nt(cg, bg)
        h_prev = h_ref[:, gs]
        y_off = _dot(cg, h_prev.astype(BF16)) * from_start_e[:, gs]
        h_ref[:, gs] = chunk_decay_e[:, gs] * h_prev + _dot(bg.astype(F32).T.astype(BF16), w_b[:, gs])
        parts = []
        for hh in range(SSD_HEADS // SSD_GROUPS):
            head = g * (SSD_HEADS // SSD_GROUPS) + hh
            c = base + head
            seg = acum[:, c:c + 1] - acum_t[c:c + 1, :]
            lmat = jnp.exp(jnp.where(tri, seg, -1e30))
            parts.append(_dot((cb * lmat).astype(BF16), xg_b[:, head * SSD_HEADDIM:(head + 1) * SSD_HEADDIM]))
        y_ref[:, gs] = (jnp.concatenate(parts, axis=1) + y_off).astype(y_ref.dtype)


def _ssd_order(step, n_ctx_chunks, n_chunks, rev):
    if not rev:
        return step
    return jnp.where(step < n_ctx_chunks, n_ctx_chunks - 1 - step, n_chunks - 1 + n_ctx_chunks - step)


def _ssd_scan(xbc_conv, misc3, dt_bias_row, a_neg_row, expand, n_ctx):
    b, s, _ = xbc_conv.shape
    q = SSD_CHUNK
    nchunks = s // q
    const = lambda shape: pl.BlockSpec(shape, lambda bi, i: (0, 0))
    specs, out_specs = [], []
    for rev in (False, True):
        order = functools.partial(_ssd_order, n_ctx_chunks=n_ctx // q, n_chunks=nchunks, rev=rev)
        chunk = lambda w, cb, order=order: pl.BlockSpec((None, q, w), lambda bi, i: (bi, order(i), cb))
        specs += [chunk(SSD_INNER, 0), chunk(SSD_GROUPS * SSD_STATE, 2), chunk(SSD_GROUPS * SSD_STATE, 3),
                  chunk(LANES, (COL_DT - COL_KR) // LANES)]
        out_specs.append(chunk(SSD_INNER, 0))
    return pl.pallas_call(
        _ssd_kernel,
        grid=(b, nchunks),
        in_specs=specs + [const((1, LANES)), const((1, LANES)), const((LANES, SSD_INNER)), const((LANES, SSD_INNER))],
        out_specs=out_specs,
        out_shape=[jax.ShapeDtypeStruct((b, s, SSD_INNER), BF16)] * 2,
        scratch_shapes=[pltpu.VMEM((SSD_STATE, SSD_INNER), F32)] * 2,
        compiler_params=_cparams(),
        name="ssd_scan",
    )(xbc_conv, xbc_conv, xbc_conv, misc3, xbc_conv, xbc_conv, xbc_conv, misc3,
      dt_bias_row, a_neg_row, expand[0], expand[1])


def _lru_kernel(x_ref, w_ref, ba_ref, bx_ref, lam_ref, o_ref, a_s, b_s, h_s, carry, *, rev):
    nb, t, _ = x_ref.shape
    per = LRU_GROUP // LANES

    @pl.when(pl.program_id(0) == 0)
    def _():
        carry[...] = jnp.zeros_like(carry)

    decay = -LRU_C * _softplus(-lam_ref[...])
    for bi in range(nb):
        for g in range(LRU_WIDTH // LRU_GROUP):
            gs = slice(g * LRU_GROUP, (g + 1) * LRU_GROUP)
            xd = x_ref[bi, :, gs]
            ri = _dot(xd, w_ref[g])
            r = _sigmoid_tanh(ri[:, :LRU_GROUP] + ba_ref[:, gs])
            i = _sigmoid_tanh(ri[:, LRU_GROUP:] + bx_ref[:, gs])
            log_a = decay[:, gs] * r
            a = jnp.exp(log_a)
            bt = jnp.sqrt(jnp.tanh(-log_a) * (1.0 + a * a)) * (i * xd.astype(F32))
            for cc in range(per):
                ls = slice(cc * LANES, (cc + 1) * LANES)
                a_s[g * per + cc, pl.ds(bi, t, stride=nb), :] = a[:, ls]
                b_s[g * per + cc, pl.ds(bi, t, stride=nb), :] = bt[:, ls]

    def step(k, h):
        tt = (t - 1 - k) if rev else k
        r0 = pl.multiple_of(tt * nb, nb)
        h = a_s[:, pl.ds(r0, nb), :] * h + b_s[:, pl.ds(r0, nb), :]
        h_s[:, pl.ds(r0, nb), :] = h
        return h

    carry[...] = lax.fori_loop(0, t, step, carry[...])
    for bi in range(nb):
        for cc in range(LRU_WIDTH // LANES):
            o_ref[bi, :, cc * LANES:(cc + 1) * LANES] = h_s[cc, pl.ds(bi, t, stride=nb), :].astype(o_ref.dtype)


def _lru_scan(lx_conv, w_gate, ba, bx, lam, n_ctx, rev):
    b, s, width = lx_conv.shape
    t = LRU_TILE
    ntiles = s // t
    order = functools.partial(_ssd_order, n_ctx_chunks=n_ctx // t, n_chunks=ntiles, rev=rev)
    const = lambda shape: pl.BlockSpec(shape, lambda i: (0,) * len(shape))
    return pl.pallas_call(
        functools.partial(_lru_kernel, rev=rev),
        grid=(ntiles,),
        in_specs=[pl.BlockSpec((b, t, width), lambda i: (0, order(i), 0)),
                  const(w_gate.shape), const((1, width)), const((1, width)), const((1, width))],
        out_specs=pl.BlockSpec((b, t, width), lambda i: (0, order(i), 0)),
        out_shape=jax.ShapeDtypeStruct((b, s, width), BF16),
        scratch_shapes=[pltpu.VMEM((width // LANES, t * b, LANES), F32)] * 3
        + [pltpu.VMEM((width // LANES, b, LANES), F32)],
        compiler_params=_cparams(),
        name="lru_rev" if rev else "lru_fwd",
    )(lx_conv, w_gate, ba, bx, lam)


def _gelu_tanh(x):
    return 0.5 * x * (1.0 + jnp.tanh(math.sqrt(2.0 / math.pi) * (x + 0.044715 * (x * x * x))))


def _merge_kernel(att_ref, y0_ref, y1_ref, xs_ref, z_ref, l0_ref, l1_ref, lg_ref, gate_ref, h_ref,
                  dskip_ref, ssdg_ref, wb_ref, wo_ref, g1_ref, lng_ref, lnb_ref, sh2_ref, sc2_ref,
                  h1_ref, u2_ref, *, alpha):
    f32 = lambda ref: ref[...].astype(F32)
    y = f32(y0_ref) + f32(y1_ref) + f32(xs_ref) * dskip_ref[...]
    y = y * _silu(f32(z_ref))
    parts = []
    for g in range(SSD_GROUPS):
        yg = y[:, g * GROUP_W:(g + 1) * GROUP_W]
        parts.append(yg * lax.rsqrt(jnp.mean(yg * yg, axis=-1, keepdims=True) + RMS_EPS))
    y_ssd = jnp.concatenate(parts, axis=1) * ssdg_ref[...]
    y_lru = (f32(l0_ref) + f32(l1_ref)) * _gelu_tanh(f32(lg_ref))
    branches = (att_ref[...], y_ssd.astype(BF16), y_lru.astype(BF16))
    mix = None
    for k in range(N_BRANCH):
        gate = gate_ref[:, k * D_MODEL:(k + 1) * D_MODEL].astype(F32)
        term = _sigmoid(gate) * _dot(branches[k], wb_ref[k])
        mix = term if mix is None else mix + term
    out = _dot(mix.astype(BF16), wo_ref[...])
    h1 = _layer_norm(alpha * h_ref[...] + g1_ref[...] * out) * lng_ref[...] + lnb_ref[...]
    h1_ref[...] = h1
    u2 = _layer_norm(h1) * (1.0 + sc2_ref[...]) + sh2_ref[...]
    for s in range(D_MODEL // LANES):
        u2_ref[:, s, :] = u2[:, s * LANES:(s + 1) * LANES]


def _merge(att, y0, y1, xbc_conv, proj, l0, l1, h, d_skip_row, ssd_gain, wb_all, wo_all, layer, mods, ln_g, ln_b,
           tps, alpha):
    n = h.shape[0]
    t = ROW_TILE
    d = D_MODEL
    row = lambda cb: pl.BlockSpec((t, d), lambda i: (i, cb))
    const = lambda shape: pl.BlockSpec(shape, lambda i: (0,) * len(shape))
    stacked = lambda shape: pl.BlockSpec((None,) + shape, lambda i: (layer,) + (0,) * len(shape))
    return pl.pallas_call(
        functools.partial(_merge_kernel, alpha=alpha),
        grid=(n // t,),
        in_specs=[row(0), row(0), row(0), row(0), row(COL_Z // d), row(0), row(0), row(COL_LG // d),
                  pl.BlockSpec((t, N_BRANCH * d), lambda i: (i, COL_GATE // (N_BRANCH * d))), row(0),
                  const((1, d)), const((1, d)), stacked(wb_all.shape[1:]), stacked(wo_all.shape[1:]),
                  _mod_spec(tps, 2), const((1, d)), const((1, d)), _mod_spec(tps, 3), _mod_spec(tps, 4)],
        out_specs=[row(0), pl.BlockSpec((t, d // LANES, LANES), lambda i: (i, 0, 0))],
        out_shape=[jax.ShapeDtypeStruct((n, d), F32),
                   jax.ShapeDtypeStruct((n, d // LANES, LANES), F32)],
        compiler_params=_cparams(),
        name="merge",
    )(att, y0, y1, xbc_conv, proj, l0, l1, proj, proj, h, d_skip_row, ssd_gain, wb_all, wo_all,
      mods, ln_g, ln_b, mods, mods)


def _rows_from_tiles(ref):
    return jnp.concatenate([ref[:, s, :] for s in range(ref.shape[1])], axis=1)


def _router_kernel(u_ref, w_ref, b_ref, meta_ref, cnt_ref, count):
    @pl.when(pl.program_id(0) == 0)
    def _():
        count[...] = jnp.zeros_like(count)

    u = _rows_from_tiles(u_ref)
    t = u.shape[0]
    uh = u.astype(BF16)
    ul = (u - uh.astype(F32)).astype(BF16)
    w = w_ref[...]
    wh = w.astype(BF16)
    wl = (w - wh.astype(F32)).astype(BF16)
    logits = _dot(uh, wh) + _dot(uh, wl) + _dot(ul, wh) + b_ref[...]
    lane = lax.broadcasted_iota(jnp.int32, logits.shape, 1)
    neg = -jnp.inf
    big = 4 * LANES

    def top1(vals):
        m = jnp.max(vals, axis=-1, keepdims=True)
        idx = jnp.min(jnp.where(vals == m, lane, big), axis=-1, keepdims=True)
        return m, idx

    glog = jnp.where(lane < N_GROUPS, logits, neg)
    gmax, gsel = top1(glog)
    gval = 1.0 / jnp.sum(jnp.exp(glog - gmax), axis=-1, keepdims=True)
    lo = N_GROUPS + gsel * EXPERTS_PER_GROUP
    elog = jnp.where(jnp.logical_and(lane >= lo, lane < lo + EXPERTS_PER_GROUP), logits, neg)
    v1, i1 = top1(elog)
    v2, i2 = top1(jnp.where(lane == i1, neg, elog))
    e21 = jnp.exp(v2 - v1)
    w1 = gval / (1.0 + e21)
    w2 = gval * e21 / (1.0 + e21)
    e1 = i1 - N_GROUPS
    e2 = i2 - N_GROUPS
    onehot = jnp.logical_or(lane == e1, lane == e2)
    oh = jnp.where(onehot, 1.0, 0.0)
    r = lax.broadcasted_iota(jnp.int32, (t, t), 0)
    c = lax.broadcasted_iota(jnp.int32, (t, t), 1)
    before = jnp.where(c < r, 1.0, 0.0).astype(BF16)
    prefix = _dot(before, oh.astype(BF16)) + count[0:1, :]
    rank1 = jnp.sum(jnp.where(lane == e1, prefix, 0.0), axis=-1, keepdims=True)
    rank2 = jnp.sum(jnp.where(lane == e2, prefix, 0.0), axis=-1, keepdims=True)
    meta = jnp.where(lane == 0, e1.astype(F32), 0.0)
    meta = jnp.where(lane == 1, e2.astype(F32), meta)
    meta = jnp.where(lane == 2, w1, meta)
    meta = jnp.where(lane == 3, w2, meta)
    meta = jnp.where(lane == 4, rank1, meta)
    meta = jnp.where(lane == 5, rank2, meta)
    meta_ref[...] = meta
    count[...] = count[...] + jnp.sum(oh, axis=0, keepdims=True)
    cnt_ref[...] = count[...]


def _router(u2t, w_router, b_router):
    n = u2t.shape[0]
    t = ROW_TILE
    return pl.pallas_call(
        _router_kernel,
        grid=(n // t,),
        in_specs=[pl.BlockSpec((t, D_MODEL // LANES, LANES), lambda i: (i, 0, 0)),
                  pl.BlockSpec((D_MODEL, LANES), lambda i: (0, 0)),
                  pl.BlockSpec((1, LANES), lambda i: (0, 0))],
        out_specs=[pl.BlockSpec((t, LANES), lambda i: (i, 0)),
                   pl.BlockSpec((SUBLANES, LANES), lambda i: (0, 0))],
        out_shape=[jax.ShapeDtypeStruct((n, LANES), F32),
                   jax.ShapeDtypeStruct((SUBLANES, LANES), F32)],
        scratch_shapes=[pltpu.VMEM((SUBLANES, LANES), F32)],
        compiler_params=_cparams(),
        name="router",
    )(u2t, w_router, b_router)


DISPATCH_TILE = 512
DMA_LOOP_UNROLL = 8


def _dispatch_kernel(dest_ref, u_ref, init_ref, x_ref, sem):
    del init_ref
    base = pl.program_id(0) * DISPATCH_TILE

    def copy(k, slot):
        return pltpu.make_async_copy(u_ref.at[k], x_ref.at[dest_ref[TOP_K * (base + k) + slot]], sem)

    def issue(k, carry):
        for slot in range(TOP_K):
            copy(k, slot).start(priority=slot)
        return carry

    def drain(k, carry):
        for slot in range(TOP_K):
            copy(k, slot).wait()
        return carry

    lax.fori_loop(0, DISPATCH_TILE, issue, 0, unroll=DMA_LOOP_UNROLL)
    lax.fori_loop(0, DISPATCH_TILE, drain, 0, unroll=DMA_LOOP_UNROLL)


def _dispatch(dest_flat, u2t, n_rows):
    n = u2t.shape[0]
    init = jnp.zeros((n_rows,) + u2t.shape[1:], u2t.dtype)
    return pl.pallas_call(
        _dispatch_kernel,
        grid_spec=pltpu.PrefetchScalarGridSpec(
            num_scalar_prefetch=1,
            grid=(n // DISPATCH_TILE,),
            in_specs=[pl.BlockSpec((DISPATCH_TILE,) + u2t.shape[1:], lambda i, dest: (i, 0, 0)),
                      pl.BlockSpec(memory_space=pl.ANY)],
            out_specs=pl.BlockSpec(memory_space=pl.ANY),
            scratch_shapes=[pltpu.SemaphoreType.DMA(())]),
        out_shape=jax.ShapeDtypeStruct(init.shape, init.dtype),
        input_output_aliases={2: 0},
        compiler_params=_cparams(has_side_effects=True),
        name="moe_dispatch",
    )(dest_flat, u2t, init)


def _expert_kernel(blk_ref, nact_ref, x_ref, w1_ref, w3_ref, w2_ref, y_ref, w1_b, w3_b, w2_b):
    i = pl.program_id(0)

    @pl.when(jnp.logical_or(i == 0, blk_ref[i] != blk_ref[jnp.maximum(i - 1, 0)]))
    def _():
        w1_b[...] = w1_ref[...].astype(BF16)
        w3_b[...] = w3_ref[...].astype(BF16)
        w2_b[...] = w2_ref[...].astype(BF16)

    @pl.when(i < nact_ref[0])
    def _():
        x = _rows_from_tiles(x_ref).astype(BF16)
        hid = _silu(_dot(x, w1_b[...])) * _dot(x, w3_b[...])
        y = _dot(hid.astype(BF16), w2_b[...])
        for s in range(D_MODEL // LANES):
            y_ref[:, s, :] = y[:, s * LANES:(s + 1) * LANES]

    @pl.when(i >= nact_ref[0])
    def _():
        y_ref[...] = jnp.zeros_like(y_ref)


def _experts(blk_e, nact, xin, w1_all, w3_all, w2_all, layer):
    nb = xin.shape[0] // EXPERT_BLOCK
    tile = (EXPERT_BLOCK, D_MODEL // LANES, LANES)
    weight = lambda r, c: pl.BlockSpec((None, None, r, c), lambda i, blk, na: (layer, blk[i], 0, 0))
    return pl.pallas_call(
        _expert_kernel,
        grid_spec=pltpu.PrefetchScalarGridSpec(
            num_scalar_prefetch=2,
            grid=(nb,),
            in_specs=[pl.BlockSpec(tile, lambda i, blk, na: (i, 0, 0)),
                      weight(D_MODEL, EXPERT_HIDDEN), weight(D_MODEL, EXPERT_HIDDEN), weight(EXPERT_HIDDEN, D_MODEL)],
            out_specs=pl.BlockSpec(tile, lambda i, blk, na: (i, 0, 0)),
            scratch_shapes=[pltpu.VMEM((D_MODEL, EXPERT_HIDDEN), BF16), pltpu.VMEM((D_MODEL, EXPERT_HIDDEN), BF16),
                            pltpu.VMEM((EXPERT_HIDDEN, D_MODEL), BF16)]),
        out_shape=jax.ShapeDtypeStruct(xin.shape, F32),
        compiler_params=_cparams(),
        name="moe_experts",
    )(blk_e, nact, xin, w1_all, w3_all, w2_all)


COMBINE_TILE = 256


def _combine_kernel(dest_ref, y_ref, meta_ref, h_ref, g2_ref, lng_ref, lnb_ref, o_ref, buf, sem, *, alpha):
    t = COMBINE_TILE
    i = pl.program_id(0)
    last = pl.num_programs(0) - 1

    def copy(step, half, k, slot):
        src = y_ref.at[dest_ref[TOP_K * (step * t + k) + slot]]
        return pltpu.make_async_copy(src, buf.at[half, slot, k], sem.at[half])

    def issue(step, half):
        def body(k, carry):
            for slot in range(TOP_K):
                copy(step, half, k, slot).start(priority=slot)
            return carry
        lax.fori_loop(0, t, body, 0, unroll=DMA_LOOP_UNROLL)

    @pl.when(i == 0)
    def _():
        issue(0, 0)

    @pl.when(i < last)
    def _():
        issue(i + 1, (i + 1) % 2)

    half = i % 2

    def drain(k, carry):
        for slot in range(TOP_K):
            copy(i, half, k, slot).wait()
        return carry

    lax.fori_loop(0, t, drain, 0, unroll=DMA_LOOP_UNROLL)
    meta = meta_ref[...]
    f = meta[:, 2:3] * _rows_from_tiles(buf.at[half, 0]) + meta[:, 3:4] * _rows_from_tiles(buf.at[half, 1])
    o_ref[...] = _layer_norm(alpha * h_ref[...] + g2_ref[...] * f) * lng_ref[...] + lnb_ref[...]


def _combine(dest_flat, yout, meta, h1, mods, ln_g, ln_b, tps, alpha):
    n = h1.shape[0]
    t = COMBINE_TILE
    d = D_MODEL

    def mod_map(i, dest):
        return (jnp.where(i % tps == 0, 8, i // tps), 0, 5)

    return pl.pallas_call(
        functools.partial(_combine_kernel, alpha=alpha),
        grid_spec=pltpu.PrefetchScalarGridSpec(
            num_scalar_prefetch=1,
            grid=(n // t,),
            in_specs=[pl.BlockSpec(memory_space=pl.ANY),
                      pl.BlockSpec((t, LANES), lambda i, dest: (i, 0)),
                      pl.BlockSpec((t, d), lambda i, dest: (i, 0)),
                      pl.BlockSpec((None, 1, d), mod_map),
                      pl.BlockSpec((1, d), lambda i, dest: (0, 0)),
                      pl.BlockSpec((1, d), lambda i, dest: (0, 0))],
            out_specs=pl.BlockSpec((t, d), lambda i, dest: (i, 0)),
            scratch_shapes=[pltpu.VMEM((2, TOP_K, t, d // LANES, LANES), F32), pltpu.SemaphoreType.DMA((2,))]),
        out_shape=jax.ShapeDtypeStruct((n, d), F32),
        compiler_params=_cparams(dimension_semantics=("arbitrary",)),
        name="moe_combine",
    )(dest_flat, yout, meta, h1, mods, ln_g, ln_b)


def _prep_w_in(w_in):
    cq, ckv, kr, z, xbc, dt, lx, lg, gate = _split_sections(w_in)
    zeros = lambda w: jnp.zeros(w_in.shape[:-1] + (w,), w_in.dtype)
    out = jnp.concatenate([z, lx, lg, gate, xbc, cq, ckv, kr, zeros(64),
                           dt[..., :SSD_HEADS], zeros(DT_DIR_STRIDE - SSD_HEADS),
                           dt[..., SSD_HEADS:], zeros(DT_DIR_STRIDE - SSD_HEADS)], axis=-1)
    return out.astype(BF16)


def _split_sections(w):
    sizes = (Q_LORA, KV_LORA, QK_ROPE, SSD_INNER, SSD_INNER + 2 * SSD_GROUPS * SSD_STATE, 2 * SSD_HEADS,
             LRU_WIDTH, LRU_WIDTH, N_BRANCH * D_MODEL)
    out, start = [], 0
    for size in sizes:
        out.append(w[..., start:start + size])
        start += size
    return out


def _dir_row(v):
    row = jnp.zeros((LANES,), F32)
    row = row.at[0:SSD_HEADS].set(v[0]).at[DT_DIR_STRIDE:DT_DIR_STRIDE + SSD_HEADS].set(v[1])
    return row[None, :]


def _expand_matrix(rev):
    rows = jnp.arange(LANES)[:, None]
    cols = jnp.arange(SSD_INNER)[None, :]
    base = DT_DIR_STRIDE if rev else 0
    return (rows - base == cols // SSD_HEADDIM).astype(BF16)


def _block_diag_gates(wa, wx):
    per = LRU_GROUP // LRU_BW
    eye = jnp.eye(per, dtype=wa.dtype)

    def bd(w):
        w = w.reshape(LRU_WIDTH // LRU_GROUP, per, LRU_BW, LRU_BW)
        return jnp.einsum('gicd,ij->gicjd', w, eye).reshape(LRU_WIDTH // LRU_GROUP, LRU_GROUP, LRU_GROUP)

    return jnp.concatenate([bd(wa), bd(wx)], axis=-1).astype(BF16)


def _rope_tables(n_ctx, seq):
    rows = seq // GRID_W
    row_pos = jnp.repeat(jnp.arange(rows, dtype=F32), GRID_W)
    col_pos = (jnp.arange(rows * GRID_W) % GRID_W).astype(F32)
    inv_freq = ROPE_THETA ** (-jnp.arange(ROPE_FREQS, dtype=F32) / ROPE_FREQS)
    ang = [row_pos[:, None] * inv_freq, col_pos[:, None] * inv_freq]
    cos = jnp.concatenate([jnp.cos(ang[0]), jnp.cos(ang[0]), jnp.cos(ang[1]), jnp.cos(ang[1])], axis=1)
    sin = jnp.concatenate([-jnp.sin(ang[0]), jnp.sin(ang[0]), -jnp.sin(ang[1]), jnp.sin(ang[1])], axis=1)
    pad = lambda t, fill: jnp.concatenate(
        [jnp.concatenate([jnp.full((n_ctx, QK_ROPE), fill, F32), t], axis=0),
         jnp.zeros((n_ctx + seq, LANES - QK_ROPE), F32)], axis=1)
    return pad(cos, 1.0), pad(sin, 0.0)


def kernel(x, c, ctx, c_ctx, w_mod, b_mod, w_in, q_norm_w, kv_norm_w, w_uq, w_ukv, ssd_conv_w, ssd_conv_b, ssd_a_log, ssd_dt_bias, ssd_d, ssd_norm_w, lru_conv_w, lru_conv_b, lru_wa, lru_ba, lru_wx, lru_bx, lru_lambda, w_branch, w_out, ln1_g, ln1_b, ln2_g, ln2_b, router_wg, router_bg, router_we, router_be, exp_w1, exp_w3, exp_w2):
    bsz, seq, d = x.shape
    n_ctx = ctx.shape[1]
    depth = w_mod.shape[0]
    assert d == D_MODEL and n_ctx == ROW_TILE and seq % ROW_TILE == 0 and bsz == SUBLANES
    s_all = n_ctx + seq
    tps = s_all // ROW_TILE
    n = bsz * s_all
    alpha = (2 * depth) ** 0.25

    h = jnp.concatenate([ctx, x], axis=1).reshape(n, d)
    cvec = jnp.zeros((16, d), F32).at[:bsz].set(c).at[bsz].set(c_ctx)
    mods_all = _mod_vectors(cvec, w_mod, b_mod).reshape(depth, 16, 1, 6 * d)
    cos_t, sin_t = _rope_tables(n_ctx, seq)
    expand = (_expand_matrix(False), _expand_matrix(True))

    w_in_p = _prep_w_in(w_in)
    wq = jnp.pad(w_uq.reshape(depth, Q_LORA, MLA_HEADS, QK_NOPE + QK_ROPE),
                 ((0, 0), (0, 0), (0, 0), (0, QK_PAD - QK_NOPE - QK_ROPE))).reshape(depth, Q_LORA, -1).astype(BF16)
    wkv4 = w_ukv.reshape(depth, KV_LORA, MLA_HEADS, QK_NOPE + V_DIM)
    wkv = jnp.concatenate([wkv4[..., :QK_NOPE].reshape(depth, KV_LORA, -1),
                           wkv4[..., QK_NOPE:].reshape(depth, KV_LORA, -1)], axis=-1).astype(BF16)
    w_router = jnp.concatenate([router_wg, router_we,
                                jnp.zeros((depth, d, LANES - N_GROUPS - N_EXPERTS), F32)], axis=-1)
    b_router = jnp.concatenate([router_bg, router_be,
                                jnp.zeros((depth, LANES - N_GROUPS - N_EXPERTS), F32)], axis=-1)
    wb_all = w_branch.astype(BF16)
    wo_all = w_out.astype(BF16)

    n_blocks = -(-(n * TOP_K) // EXPERT_BLOCK) + N_EXPERTS
    row1 = lambda v: v.reshape(1, -1)

    for l in range(depth):
        mods = mods_all[l]
        proj, misc = _in_proj(h, mods, w_in_p, l, tps)
        proj3 = proj.reshape(bsz, s_all, -1)
        misc3 = misc.reshape(bsz, s_all, -1)

        q, k, vt = _mla_prep(proj, misc, row1(q_norm_w[l]), row1(kv_norm_w[l]), wq, wkv, l, cos_t, sin_t, tps)
        att = _attention(q.reshape(bsz, s_all, -1), k.reshape(bsz, s_all, -1), vt, n_ctx)

        xbc_conv = _dwconv(proj3, COL_XBC, 2 * SSD_INNER, ssd_conv_w[l], row1(ssd_conv_b[l]), True)
        dtb = _dir_row(ssd_dt_bias[l])
        aneg = _dir_row(-jnp.exp(ssd_a_log[l]))
        ys = _ssd_scan(xbc_conv, misc3, dtb, aneg, expand, n_ctx)

        lx_conv = _dwconv(proj3, COL_LX, LRU_WIDTH, lru_conv_w[l], row1(lru_conv_b[l]), False)
        hs = [_lru_scan(lx_conv, _block_diag_gates(lru_wa[l, dr], lru_wx[l, dr]), row1(lru_ba[l, dr]),
                        row1(lru_bx[l, dr]), row1(lru_lambda[l, dr]), n_ctx, bool(dr)) for dr in range(2)]

        d_skip_row = row1(jnp.repeat(ssd_d[l], SSD_HEADDIM))
        h1, u2t = _merge(att.reshape(n, -1), ys[0].reshape(n, -1), ys[1].reshape(n, -1),
                         xbc_conv.reshape(n, -1), proj, hs[0].reshape(n, -1), hs[1].reshape(n, -1), h,
                         d_skip_row, row1(ssd_norm_w[l]), wb_all, wo_all, l, mods,
                         row1(ln1_g[l]), row1(ln1_b[l]), tps, alpha)

        meta, cnt = _router(u2t, w_router[l], row1(b_router[l]))
        counts = cnt[0, :N_EXPERTS].astype(jnp.int32)
        padded = (counts + EXPERT_BLOCK - 1) // EXPERT_BLOCK * EXPERT_BLOCK
        pend = jnp.cumsum(padded)
        pstart = pend - padded
        eid = meta[:, 0:TOP_K].astype(jnp.int32)
        rank = meta[:, 4:4 + TOP_K].astype(jnp.int32)
        dest = (jnp.take(pstart, eid) + rank).reshape(-1)
        blk_start = jnp.arange(n_blocks, dtype=jnp.int32) * EXPERT_BLOCK
        blk_e = jnp.minimum(jnp.sum(pend[None, :] <= blk_start[:, None], axis=1), N_EXPERTS - 1).astype(jnp.int32)
        nact = (pend[-1:] // EXPERT_BLOCK).astype(jnp.int32)
        blk_e = jnp.where(jnp.arange(n_blocks) < nact[0], blk_e, jnp.take(blk_e, jnp.maximum(nact[0] - 1, 0)))

        xin = _dispatch(dest, u2t, n_blocks * EXPERT_BLOCK)
        yout = _experts(blk_e, nact, xin, exp_w1, exp_w3, exp_w2, l)
        h = _combine(dest, yout, meta, h1, mods, row1(ln2_g[l]), row1(ln2_b[l]), tps, alpha)

    return h.reshape(bsz, s_all, d)[:, n_ctx:, :]
```

```python
import functools
import math

import jax
import jax.numpy as jnp
from jax import lax
from jax.experimental import pallas as pl
from jax.experimental.pallas import tpu as pltpu

F32 = jnp.float32
BF16 = jnp.bfloat16

D_MODEL = 1024
GRID_W = 64
N_BRANCH = 3

MLA_HEADS = 8
QK_NOPE = 128
QK_ROPE = 64
V_DIM = 128
Q_LORA = 512
KV_LORA = 256
ROPE_FREQS = QK_ROPE // 4
ROPE_THETA = 10000.0
ATTN_SCALE = (QK_NOPE + QK_ROPE) ** -0.5
QK_PAD = 256

SSD_HEADDIM = 64
SSD_INNER = 1024
SSD_HEADS = 16
SSD_GROUPS = 4
SSD_STATE = 128
SSD_CHUNK = 128
GROUP_W = SSD_INNER // SSD_GROUPS

CONV_W = 4
LRU_WIDTH = 1024
LRU_BW = 64
LRU_C = 8.0
LRU_TILE = 128
LRU_GROUP = 256

N_GROUPS = 4
EXPERTS_PER_GROUP = 8
N_EXPERTS = 32
TOP_K = 2
EXPERT_HIDDEN = 512
EXPERT_BLOCK = 256

LN_EPS = 1e-5
RMS_EPS = 1e-6

LANES = 128
SUBLANES = 8
ROW_TILE = 256
ROW_CHUNKS = D_MODEL // LANES
VMEM_LIMIT = 56 * 1024 * 1024

COL_Z = 0
COL_LX = 1024
COL_LG = 2048
COL_GATE = 3072
COL_XBC = 6144
COL_CQ = 8192
COL_CKV = 8704
COL_KR = 8960
COL_DT = 9088
DT_DIR_STRIDE = 64
IN_COLS_PAD = 9216


def _cparams(**kw):
    return pltpu.CompilerParams(vmem_limit_bytes=VMEM_LIMIT, **kw)


def _split3(x):
    hi = x.astype(BF16)
    r = x - hi.astype(F32)
    mid = r.astype(BF16)
    lo = (r - mid.astype(F32)).astype(BF16)
    return hi, mid, lo


def _dot(a, b):
    return jnp.dot(a, b, preferred_element_type=F32)


def _dot_nt(a, b):
    return lax.dot_general(a, b, (((1,), (1,)), ((), ())), preferred_element_type=F32)


def _sigmoid(x):
    return 1.0 / (1.0 + jnp.exp(-x))


def _sigmoid_tanh(x):
    return 0.5 * jnp.tanh(0.5 * x) + 0.5


def _silu(x):
    return x * _sigmoid(x)


def _softplus(x):
    return jnp.maximum(x, 0.0) + jnp.log(1.0 + jnp.exp(-jnp.abs(x)))


def _layer_norm(t):
    mu = jnp.mean(t, axis=-1, keepdims=True)
    c = t - mu
    var = jnp.mean(c * c, axis=-1, keepdims=True)
    return c * lax.rsqrt(var + LN_EPS)


def _mod_kernel(c_ref, w_ref, b_ref, o_ref):
    s = _silu(c_ref[...]).astype(BF16)
    o_ref[...] = _dot(s, w_ref[...].astype(BF16)) + b_ref[...]


def _mod_vectors(cvec, w_mod, b_mod):
    depth, d, n = w_mod.shape
    tn = 1536
    return pl.pallas_call(
        _mod_kernel,
        grid=(depth, n // tn),
        in_specs=[pl.BlockSpec((16, d), lambda l, j: (0, 0)),
                  pl.BlockSpec((None, d, tn), lambda l, j: (l, 0, j)),
                  pl.BlockSpec((None, 1, tn), lambda l, j: (l, 0, j))],
        out_specs=pl.BlockSpec((None, 16, tn), lambda l, j: (l, 0, j)),
        out_shape=jax.ShapeDtypeStruct((depth, 16, n), F32),
        compiler_params=_cparams(),
        name="mod_vectors",
    )(cvec, w_mod, b_mod.reshape(depth, 1, n))


def _mod_spec(tiles_per_sample, k):
    def imap(i):
        return (jnp.where(i % tiles_per_sample == 0, 8, i // tiles_per_sample), 0, k)
    return pl.BlockSpec((None, 1, D_MODEL), imap)


IN_PROJ_COLS = 1024


def _in_proj_kernel(h_ref, sh_ref, sc_ref, w_ref, o_ref, misc_ref):
    u = (_layer_norm(h_ref[...]) * (1.0 + sc_ref[...]) + sh_ref[...]).astype(BF16)
    main = o_ref.shape[1]
    for c0 in range(0, main, IN_PROJ_COLS):
        cs = slice(c0, min(c0 + IN_PROJ_COLS, main))
        o_ref[:, cs] = _dot(u, w_ref[:, cs]).astype(o_ref.dtype)
    misc_ref[...] = _dot(u, w_ref[:, main:])


def _in_proj(h, mods, w_all, layer, tps):
    n = h.shape[0]
    _, k, cols = w_all.shape
    return pl.pallas_call(
        _in_proj_kernel,
        grid=(n // ROW_TILE,),
        in_specs=[pl.BlockSpec((ROW_TILE, k), lambda i: (i, 0)),
                  _mod_spec(tps, 0), _mod_spec(tps, 1),
                  pl.BlockSpec((None, k, cols), lambda i: (layer, 0, 0), pipeline_mode=pl.Buffered(1))],
        out_specs=[pl.BlockSpec((ROW_TILE, COL_KR), lambda i: (i, 0)),
                   pl.BlockSpec((ROW_TILE, cols - COL_KR), lambda i: (i, 0))],
        out_shape=[jax.ShapeDtypeStruct((n, COL_KR), BF16),
                   jax.ShapeDtypeStruct((n, cols - COL_KR), F32)],
        compiler_params=_cparams(),
        name="in_proj",
    )(h, mods, mods, w_all)


def _rope128(t, cos, sin):
    lane = lax.broadcasted_iota(jnp.int32, t.shape, 1)
    partner = jnp.where(lane % (2 * ROPE_FREQS) < ROPE_FREQS,
                        pltpu.roll(t, LANES - ROPE_FREQS, 1), pltpu.roll(t, ROPE_FREQS, 1))
    return t * cos + partner * sin


def _rms(t, gain):
    return t * lax.rsqrt(jnp.mean(t * t, axis=-1, keepdims=True) + RMS_EPS) * gain


def _mla_prep_kernel(cq_ref, ckv_ref, kr_ref, qg_ref, kvg_ref, wq_ref, wkv_ref, cos_ref, sin_ref,
                     q_ref, k_ref, vt_ref):
    cos = cos_ref[...]
    sin = sin_ref[...]
    q = _dot(_rms(cq_ref[...].astype(F32), qg_ref[...]).astype(BF16), wq_ref[...]) * ATTN_SCALE
    kv = _dot(_rms(ckv_ref[...].astype(F32), kvg_ref[...]).astype(BF16), wkv_ref[...])
    krz = _rope128(kr_ref[...], cos, sin).astype(BF16)
    for h in range(MLA_HEADS):
        c0 = h * QK_PAD
        q_ref[:, c0:c0 + QK_NOPE] = q[:, c0:c0 + QK_NOPE].astype(BF16)
        q_ref[:, c0 + QK_NOPE:c0 + QK_PAD] = _rope128(q[:, c0 + QK_NOPE:c0 + QK_PAD], cos, sin).astype(BF16)
        k_ref[:, c0:c0 + QK_NOPE] = kv[:, h * QK_NOPE:(h + 1) * QK_NOPE].astype(BF16)
        k_ref[:, c0 + QK_NOPE:c0 + QK_PAD] = krz
    vt_ref[...] = kv[:, MLA_HEADS * QK_NOPE:].T.astype(BF16)


def _mla_prep(proj, misc, q_gain, kv_gain, wq_all, wkv_all, layer, cos_t, sin_t, tps):
    n = proj.shape[0]
    t = ROW_TILE
    row = lambda w, cb: pl.BlockSpec((t, w), lambda i: (i, cb))
    const = lambda shape: pl.BlockSpec(shape, lambda i: (0, 0))
    stacked = lambda shape: pl.BlockSpec((None,) + shape, lambda i: (layer, 0, 0))
    return pl.pallas_call(
        _mla_prep_kernel,
        grid=(n // t,),
        in_specs=[row(Q_LORA, COL_CQ // Q_LORA), row(KV_LORA, COL_CKV // KV_LORA), row(LANES, 0),
                  const((1, Q_LORA)), const((1, KV_LORA)),
                  stacked((Q_LORA, MLA_HEADS * QK_PAD)), stacked((KV_LORA, 2 * MLA_HEADS * QK_NOPE)),
                  pl.BlockSpec((t, LANES), lambda i: (i % tps, 0)),
                  pl.BlockSpec((t, LANES), lambda i: (i % tps, 0))],
        out_specs=[pl.BlockSpec((t, MLA_HEADS * QK_PAD), lambda i: (i, 0)),
                   pl.BlockSpec((t, MLA_HEADS * QK_PAD), lambda i: (i, 0)),
                   pl.BlockSpec((None, MLA_HEADS * V_DIM, t), lambda i: (i // tps, 0, i % tps))],
        out_shape=[jax.ShapeDtypeStruct((n, MLA_HEADS * QK_PAD), BF16),
                   jax.ShapeDtypeStruct((n, MLA_HEADS * QK_PAD), BF16),
                   jax.ShapeDtypeStruct((n // (tps * t), MLA_HEADS * V_DIM, tps * t), BF16)],
        compiler_params=_cparams(),
        name="mla_prep",
    )(proj, proj, misc, q_gain, kv_gain, wq_all, wkv_all, cos_t, sin_t)


ATTN_KEY_CHUNK = 768


def _attn_kernel(q_ref, k_ref, vt_ref, o_ref, s0_ref, s1_ref, *, n_ctx):
    s_all = k_ref.shape[0]
    tq = ROW_TILE
    n_tiles = (s_all - n_ctx) // tq
    chunks = [(c, min(c + ATTN_KEY_CHUNK, s_all)) for c in range(0, s_all, ATTN_KEY_CHUNK)]

    def scores(s_ref, r0, key_chunks):
        q = q_ref[pl.ds(r0, tq), :]
        for c0, c1 in key_chunks:
            s_ref[c0:c1, :] = _dot_nt(k_ref[c0:c1, :], q)

    def finish(s_ref, r0, key_chunks):
        m = None
        for c0, c1 in key_chunks:
            cm = jnp.max(s_ref[c0:c1, :], axis=0, keepdims=True)
            m = cm if m is None else jnp.maximum(m, cm)
        l = acc = None
        for c0, c1 in key_chunks:
            p = jnp.exp(s_ref[c0:c1, :] - m)
            ps = jnp.sum(p, axis=0, keepdims=True)
            pv = _dot(vt_ref[:, c0:c1], p.astype(BF16))
            l, acc = (ps, pv) if l is None else (l + ps, acc + pv)
        o_ref[pl.ds(r0, tq), :] = (acc / l).T.astype(o_ref.dtype)

    scores(s0_ref, 0, [(0, n_ctx)])
    finish(s0_ref, 0, [(0, n_ctx)])
    scores(s0_ref, n_ctx, chunks)

    def body(j, carry):
        r_a = pl.multiple_of(n_ctx + 2 * j * tq, tq)
        r_b = pl.multiple_of(r_a + tq, tq)
        r_c = pl.multiple_of(jnp.minimum(r_b + tq, s_all - tq), tq)
        scores(s1_ref, r_b, chunks)
        finish(s0_ref, r_a, chunks)
        scores(s0_ref, r_c, chunks)
        finish(s1_ref, r_b, chunks)
        return carry

    assert n_tiles % 2 == 0
    lax.fori_loop(0, n_tiles // 2, body, 0)


def _attention(q, k, vt, n_ctx):
    b, s, _ = q.shape
    return pl.pallas_call(
        functools.partial(_attn_kernel, n_ctx=n_ctx),
        grid=(b, MLA_HEADS),
        in_specs=[pl.BlockSpec((None, s, QK_PAD), lambda bi, h: (bi, 0, h)),
                  pl.BlockSpec((None, s, QK_PAD), lambda bi, h: (bi, 0, h)),
                  pl.BlockSpec((None, V_DIM, s), lambda bi, h: (bi, h, 0))],
        out_specs=pl.BlockSpec((None, s, V_DIM), lambda bi, h: (bi, 0, h)),
        out_shape=jax.ShapeDtypeStruct((b, s, MLA_HEADS * V_DIM), BF16),
        scratch_shapes=[pltpu.VMEM((s, ROW_TILE), F32), pltpu.VMEM((s, ROW_TILE), F32)],
        compiler_params=_cparams(),
        name="attention",
    )(q, k, vt)


def _conv_kernel(x_ref, p_ref, n_ref, w_ref, b_ref, o_ref, *, silu, tps):
    j = pl.program_id(1)
    has_prev = (j > 1).astype(F32)
    has_next = jnp.logical_and(j > 0, j < tps - 1).astype(F32)
    x = x_ref[...].astype(F32)
    xe = jnp.concatenate([p_ref[...].astype(F32) * has_prev, x, n_ref[...].astype(F32) * has_next], axis=0)
    rows = xe.shape[0]
    t = x_ref.shape[0]
    w = w_ref[...]
    y = b_ref[...] + w[2:3, :] * x
    y = y + w[0:1, :] * pltpu.roll(xe, 2, 0)[CONV_HALO:CONV_HALO + t]
    y = y + w[1:2, :] * pltpu.roll(xe, 1, 0)[CONV_HALO:CONV_HALO + t]
    y = y + w[3:4, :] * pltpu.roll(xe, rows - 1, 0)[CONV_HALO:CONV_HALO + t]
    o_ref[...] = (_silu(y) if silu else y).astype(o_ref.dtype)


CONV_HALO = 16


def _dwconv(proj3, col0, width, w, bias, silu):
    b, s, _ = proj3.shape
    t = ROW_TILE
    tc = 1024
    cb = col0 // tc
    tps = s // t
    hb = t // CONV_HALO
    last = s // CONV_HALO - 1
    return pl.pallas_call(
        functools.partial(_conv_kernel, silu=silu, tps=tps),
        grid=(b, tps, width // tc),
        in_specs=[pl.BlockSpec((None, t, tc), lambda bi, j, c: (bi, j, cb + c)),
                  pl.BlockSpec((None, CONV_HALO, tc), lambda bi, j, c: (bi, jnp.maximum(j * hb - 1, 0), cb + c)),
                  pl.BlockSpec((None, CONV_HALO, tc), lambda bi, j, c: (bi, jnp.minimum((j + 1) * hb, last), cb + c)),
                  pl.BlockSpec((CONV_W, tc), lambda bi, j, c: (0, c)),
                  pl.BlockSpec((1, tc), lambda bi, j, c: (0, c))],
        out_specs=pl.BlockSpec((None, t, tc), lambda bi, j, c: (bi, j, c)),
        out_shape=jax.ShapeDtypeStruct((b, s, width), BF16),
        compiler_params=_cparams(),
        name="dwconv",
    )(proj3, proj3, proj3, w, bias)


def _ssd_kernel(xs0, b0, c0, dt0, xs1, b1, c1, dt1, dtb_ref, aneg_ref, e0_ref, e1_ref, y0_ref, y1_ref,
                h0_ref, h1_ref):
    @pl.when(pl.program_id(1) == 0)
    def _():
        h0_ref[...] = jnp.zeros_like(h0_ref)
        h1_ref[...] = jnp.zeros_like(h1_ref)

    _ssd_chunk(xs0, b0, c0, dt0, dtb_ref, aneg_ref, e0_ref, y0_ref, h0_ref, rev=False)
    _ssd_chunk(xs1, b1, c1, dt1, dtb_ref, aneg_ref, e1_ref, y1_ref, h1_ref, rev=True)


def _ssd_chunk(xs_ref, b_ref, c_ref, dt_ref, dtb_ref, aneg_ref, e_ref, y_ref, h_ref, *, rev):
    q = SSD_CHUNK
    dt = _softplus(dt_ref[...] + dtb_ref[...])
    a = dt * aneg_ref[...]
    row = lax.broadcasted_iota(jnp.int32, (q, q), 0)
    col = lax.broadcasted_iota(jnp.int32, (q, q), 1)
    tri = (col >= row) if rev else (col <= row)
    tri_b = jnp.where(tri, 1.0, 0.0).astype(BF16)
    a3 = _split3(a)
    acum = _dot(tri_b, a3[0]) + _dot(tri_b, a3[1]) + _dot(tri_b, a3[2])
    total = jnp.sum(a, axis=0, keepdims=True)
    stack = jnp.concatenate([dt, total - acum, acum, jnp.broadcast_to(total, (SUBLANES, LANES))], axis=0)
    e = e_ref[...]
    s3 = _split3(stack)
    ex = _dot(s3[0], e) + _dot(s3[1], e) + _dot(s3[2], e)
    dt_e = ex[0:q]
    to_end_e = jnp.exp(ex[q:2 * q])
    from_start_e = jnp.exp(ex[2 * q:3 * q])
    chunk_decay_e = jnp.exp(ex[3 * q:3 * q + 1])
    xg = xs_ref[...].astype(F32) * dt_e
    xg_b = xg.astype(BF16)
    w_b = (xg * to_end_e).astype(BF16)
    acum_t = acum.T
    base = DT_DIR_STRIDE if rev else 0
    for g in range(SSD_GROUPS):
        gs = slice(g * GROUP_W, (g + 1) * GROUP_W)
        bg = b_ref[:, g * SSD_STATE:(g + 1) * SSD_STATE]
        cg = c_ref[:, g * SSD_STATE:(g + 1) * SSD_STATE]
        cb = _dot_nt(cg, bg)
        h_prev = h_ref[:, gs]
        y_off = _dot(cg, h_prev.astype(BF16)) * from_start_e[:, gs]
        h_ref[:, gs] = chunk_decay_e[:, gs] * h_prev + _dot(bg.astype(F32).T.astype(BF16), w_b[:, gs])
        parts = []
        for hh in range(SSD_HEADS // SSD_GROUPS):
            head = g * (SSD_HEADS // SSD_GROUPS) + hh
            c = base + head
            seg = acum[:, c:c + 1] - acum_t[c:c + 1, :]
            lmat = jnp.exp(jnp.where(tri, seg, -1e30))
            parts.append(_dot((cb * lmat).astype(BF16), xg_b[:, head * SSD_HEADDIM:(head + 1) * SSD_HEADDIM]))
        y_ref[:, gs] = (jnp.concatenate(parts, axis=1) + y_off).astype(y_ref.dtype)


def _ssd_order(step, n_ctx_chunks, n_chunks, rev):
    if not rev:
        return step
    return jnp.where(step < n_ctx_chunks, n_ctx_chunks - 1 - step, n_chunks - 1 + n_ctx_chunks - step)


def _ssd_scan(xbc_conv, misc3, dt_bias_row, a_neg_row, expand, n_ctx):
    b, s, _ = xbc_conv.shape
    q = SSD_CHUNK
    nchunks = s // q
    const = lambda shape: pl.BlockSpec(shape, lambda bi, i: (0, 0))
    specs, out_specs = [], []
    for rev in (False, True):
        order = functools.partial(_ssd_order, n_ctx_chunks=n_ctx // q, n_chunks=nchunks, rev=rev)
        chunk = lambda w, cb, order=order: pl.BlockSpec((None, q, w), lambda bi, i: (bi, order(i), cb))
        specs += [chunk(SSD_INNER, 0), chunk(SSD_GROUPS * SSD_STATE, 2), chunk(SSD_GROUPS * SSD_STATE, 3),
                  chunk(LANES, (COL_DT - COL_KR) // LANES)]
        out_specs.append(chunk(SSD_INNER, 0))
    return pl.pallas_call(
        _ssd_kernel,
        grid=(b, nchunks),
        in_specs=specs + [const((1, LANES)), const((1, LANES)), const((LANES, SSD_INNER)), const((LANES, SSD_INNER))],
        out_specs=out_specs,
        out_shape=[jax.ShapeDtypeStruct((b, s, SSD_INNER), BF16)] * 2,
        scratch_shapes=[pltpu.VMEM((SSD_STATE, SSD_INNER), F32)] * 2,
        compiler_params=_cparams(),
        name="ssd_scan",
    )(xbc_conv, xbc_conv, xbc_conv, misc3, xbc_conv, xbc_conv, xbc_conv, misc3,
      dt_bias_row, a_neg_row, expand[0], expand[1])


def _lru_kernel(x_ref, w_ref, ba_ref, bx_ref, lam_ref, o_ref, a_s, b_s, h_s, carry, *, rev):
    nb, t, _ = x_ref.shape
    per = LRU_GROUP // LANES

    @pl.when(pl.program_id(0) == 0)
    def _():
        carry[...] = jnp.zeros_like(carry)

    decay = -LRU_C * _softplus(-lam_ref[...])
    for bi in range(nb):
        for g in range(LRU_WIDTH // LRU_GROUP):
            gs = slice(g * LRU_GROUP, (g + 1) * LRU_GROUP)
            xd = x_ref[bi, :, gs]
            ri = _dot(xd, w_ref[g])
            r = _sigmoid_tanh(ri[:, :LRU_GROUP] + ba_ref[:, gs])
            i = _sigmoid_tanh(ri[:, LRU_GROUP:] + bx_ref[:, gs])
            log_a = decay[:, gs] * r
            a = jnp.exp(log_a)
            bt = jnp.sqrt(jnp.tanh(-log_a) * (1.0 + a * a)) * (i * xd.astype(F32))
            for cc in range(per):
                ls = slice(cc * LANES, (cc + 1) * LANES)
                a_s[g * per + cc, pl.ds(bi, t, stride=nb), :] = a[:, ls]
                b_s[g * per + cc, pl.ds(bi, t, stride=nb), :] = bt[:, ls]

    def step(k, h):
        tt = (t - 1 - k) if rev else k
        r0 = pl.multiple_of(tt * nb, nb)
        h = a_s[:, pl.ds(r0, nb), :] * h + b_s[:, pl.ds(r0, nb), :]
        h_s[:, pl.ds(r0, nb), :] = h
        return h

    carry[...] = lax.fori_loop(0, t, step, carry[...])
    for bi in range(nb):
        for cc in range(LRU_WIDTH // LANES):
            o_ref[bi, :, cc * LANES:(cc + 1) * LANES] = h_s[cc, pl.ds(bi, t, stride=nb), :].astype(o_ref.dtype)


def _lru_scan(lx_conv, w_gate, ba, bx, lam, n_ctx, rev):
    b, s, width = lx_conv.shape
    t = LRU_TILE
    ntiles = s // t
    order = functools.partial(_ssd_order, n_ctx_chunks=n_ctx // t, n_chunks=ntiles, rev=rev)
    const = lambda shape: pl.BlockSpec(shape, lambda i: (0,) * len(shape))
    return pl.pallas_call(
        functools.partial(_lru_kernel, rev=rev),
        grid=(ntiles,),
        in_specs=[pl.BlockSpec((b, t, width), lambda i: (0, order(i), 0)),
                  const(w_gate.shape), const((1, width)), const((1, width)), const((1, width))],
        out_specs=pl.BlockSpec((b, t, width), lambda i: (0, order(i), 0)),
        out_shape=jax.ShapeDtypeStruct((b, s, width), BF16),
        scratch_shapes=[pltpu.VMEM((width // LANES, t * b, LANES), F32)] * 3
        + [pltpu.VMEM((width // LANES, b, LANES), F32)],
        compiler_params=_cparams(),
        name="lru_rev" if rev else "lru_fwd",
    )(lx_conv, w_gate, ba, bx, lam)


def _gelu_tanh(x):
    return 0.5 * x * (1.0 + jnp.tanh(math.sqrt(2.0 / math.pi) * (x + 0.044715 * (x * x * x))))


def _merge_kernel(att_ref, y0_ref, y1_ref, xs_ref, z_ref, l0_ref, l1_ref, lg_ref, gate_ref, h_ref,
                  dskip_ref, ssdg_ref, wb_ref, wo_ref, g1_ref, lng_ref, lnb_ref, sh2_ref, sc2_ref,
                  h1_ref, u2_ref, *, alpha):
    f32 = lambda ref: ref[...].astype(F32)
    y = f32(y0_ref) + f32(y1_ref) + f32(xs_ref) * dskip_ref[...]
    y = y * _silu(f32(z_ref))
    parts = []
    for g in range(SSD_GROUPS):
        yg = y[:, g * GROUP_W:(g + 1) * GROUP_W]
        parts.append(yg * lax.rsqrt(jnp.mean(yg * yg, axis=-1, keepdims=True) + RMS_EPS))
    y_ssd = jnp.concatenate(parts, axis=1) * ssdg_ref[...]
    y_lru = (f32(l0_ref) + f32(l1_ref)) * _gelu_tanh(f32(lg_ref))
    branches = (att_ref[...], y_ssd.astype(BF16), y_lru.astype(BF16))
    mix = None
    for k in range(N_BRANCH):
        gate = gate_ref[:, k * D_MODEL:(k + 1) * D_MODEL].astype(F32)
        term = _sigmoid(gate) * _dot(branches[k], wb_ref[k])
        mix = term if mix is None else mix + term
    out = _dot(mix.astype(BF16), wo_ref[...])
    h1 = _layer_norm(alpha * h_ref[...] + g1_ref[...] * out) * lng_ref[...] + lnb_ref[...]
    h1_ref[...] = h1
    u2 = _layer_norm(h1) * (1.0 + sc2_ref[...]) + sh2_ref[...]
    for s in range(D_MODEL // LANES):
        u2_ref[pl.ds(s, u2.shape[0], stride=ROW_CHUNKS), :] = u2[:, s * LANES:(s + 1) * LANES]


def _merge(att, y0, y1, xbc_conv, proj, l0, l1, h, d_skip_row, ssd_gain, wb_all, wo_all, layer, mods, ln_g, ln_b,
           tps, alpha):
    n = h.shape[0]
    t = ROW_TILE
    d = D_MODEL
    row = lambda cb: pl.BlockSpec((t, d), lambda i: (i, cb))
    const = lambda shape: pl.BlockSpec(shape, lambda i: (0,) * len(shape))
    stacked = lambda shape: pl.BlockSpec((None,) + shape, lambda i: (layer,) + (0,) * len(shape))
    return pl.pallas_call(
        functools.partial(_merge_kernel, alpha=alpha),
        grid=(n // t,),
        in_specs=[row(0), row(0), row(0), row(0), row(COL_Z // d), row(0), row(0), row(COL_LG // d),
                  pl.BlockSpec((t, N_BRANCH * d), lambda i: (i, COL_GATE // (N_BRANCH * d))), row(0),
                  const((1, d)), const((1, d)), stacked(wb_all.shape[1:]), stacked(wo_all.shape[1:]),
                  _mod_spec(tps, 2), const((1, d)), const((1, d)), _mod_spec(tps, 3), _mod_spec(tps, 4)],
        out_specs=[row(0), pl.BlockSpec((t * ROW_CHUNKS, LANES), lambda i: (i, 0))],
        out_shape=[jax.ShapeDtypeStruct((n, d), F32),
                   jax.ShapeDtypeStruct((n * ROW_CHUNKS, LANES), F32)],
        compiler_params=_cparams(),
        name="merge",
    )(att, y0, y1, xbc_conv, proj, l0, l1, proj, proj, h, d_skip_row, ssd_gain, wb_all, wo_all,
      mods, ln_g, ln_b, mods, mods)


def _row_tile(ref, r):
    return ref.at[pl.ds(pl.multiple_of(r * ROW_CHUNKS, ROW_CHUNKS), ROW_CHUNKS), :]


def _rows_from_tiles(ref):
    rows = ref.shape[0] // ROW_CHUNKS
    return jnp.concatenate([ref[pl.ds(s, rows, stride=ROW_CHUNKS), :] for s in range(ROW_CHUNKS)], axis=1)


def _router_kernel(u_ref, w_ref, b_ref, meta_ref, meta_t_ref, cnt_ref, count):
    @pl.when(pl.program_id(0) == 0)
    def _():
        count[...] = jnp.zeros_like(count)

    u = _rows_from_tiles(u_ref)
    t = u.shape[0]
    uh = u.astype(BF16)
    ul = (u - uh.astype(F32)).astype(BF16)
    w = w_ref[...]
    wh = w.astype(BF16)
    wl = (w - wh.astype(F32)).astype(BF16)
    logits = _dot(uh, wh) + _dot(uh, wl) + _dot(ul, wh) + b_ref[...]
    lane = lax.broadcasted_iota(jnp.int32, logits.shape, 1)
    neg = -jnp.inf
    big = 4 * LANES

    def top1(vals):
        m = jnp.max(vals, axis=-1, keepdims=True)
        idx = jnp.min(jnp.where(vals == m, lane, big), axis=-1, keepdims=True)
        return m, idx

    glog = jnp.where(lane < N_GROUPS, logits, neg)
    gmax, gsel = top1(glog)
    gval = 1.0 / jnp.sum(jnp.exp(glog - gmax), axis=-1, keepdims=True)
    lo = N_GROUPS + gsel * EXPERTS_PER_GROUP
    elog = jnp.where(jnp.logical_and(lane >= lo, lane < lo + EXPERTS_PER_GROUP), logits, neg)
    v1, i1 = top1(elog)
    v2, i2 = top1(jnp.where(lane == i1, neg, elog))
    e21 = jnp.exp(v2 - v1)
    w1 = gval / (1.0 + e21)
    w2 = gval * e21 / (1.0 + e21)
    e1 = i1 - N_GROUPS
    e2 = i2 - N_GROUPS
    onehot = jnp.logical_or(lane == e1, lane == e2)
    oh = jnp.where(onehot, 1.0, 0.0)
    r = lax.broadcasted_iota(jnp.int32, (t, t), 0)
    c = lax.broadcasted_iota(jnp.int32, (t, t), 1)
    before = jnp.where(c < r, 1.0, 0.0).astype(BF16)
    prefix = _dot(before, oh.astype(BF16)) + count[0:1, :]
    rank1 = jnp.sum(jnp.where(lane == e1, prefix, 0.0), axis=-1, keepdims=True)
    rank2 = jnp.sum(jnp.where(lane == e2, prefix, 0.0), axis=-1, keepdims=True)
    meta = jnp.where(lane == 0, e1.astype(F32), 0.0)
    meta = jnp.where(lane == 1, e2.astype(F32), meta)
    meta = jnp.where(lane == 2, w1, meta)
    meta = jnp.where(lane == 3, w2, meta)
    meta = jnp.where(lane == 4, rank1, meta)
    meta = jnp.where(lane == 5, rank2, meta)
    meta_ref[...] = meta
    meta_t_ref[...] = meta.T[0:SUBLANES, :]
    count[...] = count[...] + jnp.sum(oh, axis=0, keepdims=True)
    cnt_ref[...] = count[...]


def _router(u2t, w_router, b_router):
    n = u2t.shape[0] // ROW_CHUNKS
    t = ROW_TILE
    return pl.pallas_call(
        _router_kernel,
        grid=(n // t,),
        in_specs=[pl.BlockSpec((t * ROW_CHUNKS, LANES), lambda i: (i, 0)),
                  pl.BlockSpec((D_MODEL, LANES), lambda i: (0, 0)),
                  pl.BlockSpec((1, LANES), lambda i: (0, 0))],
        out_specs=[pl.BlockSpec((t, LANES), lambda i: (i, 0)),
                   pl.BlockSpec((SUBLANES, t), lambda i: (0, i)),
                   pl.BlockSpec((SUBLANES, LANES), lambda i: (0, 0))],
        out_shape=[jax.ShapeDtypeStruct((n, LANES), F32),
                   jax.ShapeDtypeStruct((SUBLANES, n), F32),
                   jax.ShapeDtypeStruct((SUBLANES, LANES), F32)],
        scratch_shapes=[pltpu.VMEM((SUBLANES, LANES), F32)],
        compiler_params=_cparams(),
        name="router",
    )(u2t, w_router, b_router)


DISPATCH_TILE = 512
DMA_LOOP_UNROLL = 8


def _dispatch_kernel(dest_ref, pend_ref, u_ref, x_ref, zeros, sem, zsem):
    base = pl.program_id(0) * DISPATCH_TILE

    @pl.when(pl.program_id(0) == 0)
    def _():
        zeros[...] = jnp.zeros_like(zeros)

        def tail(e):
            start = pl.multiple_of((pend_ref[e] - EXPERT_BLOCK) * ROW_CHUNKS, EXPERT_BLOCK * ROW_CHUNKS)
            return pltpu.make_async_copy(zeros, x_ref.at[pl.ds(start, EXPERT_BLOCK * ROW_CHUNKS), :], zsem)

        def nonempty(e):
            return pend_ref[e] > jnp.where(e == 0, 0, pend_ref[jnp.maximum(e - 1, 0)])

        def fill(e, carry):
            @pl.when(nonempty(e))
            def _():
                tail(e).start()
            return carry

        def done(e, carry):
            @pl.when(nonempty(e))
            def _():
                tail(e).wait()
            return carry

        lax.fori_loop(0, N_EXPERTS, fill, 0)
        lax.fori_loop(0, N_EXPERTS, done, 0)

        n_blocks = x_ref.shape[0] // (EXPERT_BLOCK * ROW_CHUNKS)
        n_active = pend_ref[N_EXPERTS - 1] // EXPERT_BLOCK

        def unused(b):
            start = pl.multiple_of(b * (EXPERT_BLOCK * ROW_CHUNKS), EXPERT_BLOCK * ROW_CHUNKS)
            return pltpu.make_async_copy(zeros, x_ref.at[pl.ds(start, EXPERT_BLOCK * ROW_CHUNKS), :], zsem)

        lax.fori_loop(n_active, n_blocks, lambda b, c: (unused(b).start(), c)[1], 0)
        lax.fori_loop(n_active, n_blocks, lambda b, c: (unused(b).wait(), c)[1], 0)

    def copy(k, slot):
        n_tok = dest_ref.shape[0] // TOP_K
        return pltpu.make_async_copy(_row_tile(u_ref, k), _row_tile(x_ref, dest_ref[slot * n_tok + base + k]), sem)

    def issue(k, carry):
        for slot in range(TOP_K):
            copy(k, slot).start(priority=slot)
        return carry

    def drain(k, carry):
        for slot in range(TOP_K):
            copy(k, slot).wait()
        return carry

    lax.fori_loop(0, DISPATCH_TILE, issue, 0, unroll=DMA_LOOP_UNROLL)
    lax.fori_loop(0, DISPATCH_TILE, drain, 0, unroll=DMA_LOOP_UNROLL)


def _dispatch(dest_flat, pend, u2t, n_rows):
    n = u2t.shape[0] // ROW_CHUNKS
    return pl.pallas_call(
        _dispatch_kernel,
        grid_spec=pltpu.PrefetchScalarGridSpec(
            num_scalar_prefetch=2,
            grid=(n // DISPATCH_TILE,),
            in_specs=[pl.BlockSpec((DISPATCH_TILE * ROW_CHUNKS, LANES), lambda i, dest, pend: (i, 0))],
            out_specs=pl.BlockSpec(memory_space=pl.ANY),
            scratch_shapes=[pltpu.VMEM((EXPERT_BLOCK * ROW_CHUNKS, LANES), u2t.dtype),
                            pltpu.SemaphoreType.DMA(()), pltpu.SemaphoreType.DMA(())]),
        out_shape=jax.ShapeDtypeStruct((n_rows * ROW_CHUNKS, LANES), u2t.dtype),
        compiler_params=_cparams(dimension_semantics=("arbitrary",)),
        name="moe_dispatch",
    )(dest_flat, pend, u2t)


def _expert_kernel(blk_ref, nact_ref, x_ref, w1_ref, w3_ref, w2_ref, y_ref, w1_b, w3_b, w2_b):
    i = pl.program_id(0)

    @pl.when(jnp.logical_or(i == 0, blk_ref[i] != blk_ref[jnp.maximum(i - 1, 0)]))
    def _():
        w1_b[...] = w1_ref[...].astype(BF16)
        w3_b[...] = w3_ref[...].astype(BF16)
        w2_b[...] = w2_ref[...].astype(BF16)

    @pl.when(i < nact_ref[0])
    def _():
        x = _rows_from_tiles(x_ref).astype(BF16)
        hid = _silu(_dot(x, w1_b[...])) * _dot(x, w3_b[...])
        y = _dot(hid.astype(BF16), w2_b[...])
        for s in range(ROW_CHUNKS):
            y_ref[pl.ds(s, EXPERT_BLOCK, stride=ROW_CHUNKS), :] = y[:, s * LANES:(s + 1) * LANES]

    @pl.when(i >= nact_ref[0])
    def _():
        y_ref[...] = jnp.zeros_like(y_ref)


def _experts(blk_e, nact, xin, w1_all, w3_all, w2_all, layer):
    nb = xin.shape[0] // (EXPERT_BLOCK * ROW_CHUNKS)
    tile = (EXPERT_BLOCK * ROW_CHUNKS, LANES)
    weight = lambda r, c: pl.BlockSpec((None, None, r, c), lambda i, blk, na: (layer, blk[i], 0, 0))
    return pl.pallas_call(
        _expert_kernel,
        grid_spec=pltpu.PrefetchScalarGridSpec(
            num_scalar_prefetch=2,
            grid=(nb,),
            in_specs=[pl.BlockSpec(tile, lambda i, blk, na: (jnp.minimum(i, na[0] - 1), 0)),
                      weight(D_MODEL, EXPERT_HIDDEN), weight(D_MODEL, EXPERT_HIDDEN), weight(EXPERT_HIDDEN, D_MODEL)],
            out_specs=pl.BlockSpec(tile, lambda i, blk, na: (i, 0)),
            scratch_shapes=[pltpu.VMEM((D_MODEL, EXPERT_HIDDEN), BF16), pltpu.VMEM((D_MODEL, EXPERT_HIDDEN), BF16),
                            pltpu.VMEM((EXPERT_HIDDEN, D_MODEL), BF16)]),
        out_shape=jax.ShapeDtypeStruct(xin.shape, F32),
        compiler_params=_cparams(),
        name="moe_experts",
    )(blk_e, nact, xin, w1_all, w3_all, w2_all)


COMBINE_TILE = 256


def _combine_kernel(dest_ref, y_ref, meta_ref, h_ref, g2_ref, lng_ref, lnb_ref, o_ref, buf, sem, *, alpha):
    t = COMBINE_TILE
    i = pl.program_id(0)
    last = pl.num_programs(0) - 1

    def copy(step, half, k, slot):
        src = _row_tile(y_ref, dest_ref[slot * (dest_ref.shape[0] // TOP_K) + step * t + k])
        return pltpu.make_async_copy(src, _row_tile(buf.at[half, slot], k), sem.at[half])

    def issue(step, half):
        def body(k, carry):
            for slot in range(TOP_K):
                copy(step, half, k, slot).start(priority=slot)
            return carry
        lax.fori_loop(0, t, body, 0, unroll=DMA_LOOP_UNROLL)

    @pl.when(i == 0)
    def _():
        issue(0, 0)

    @pl.when(i < last)
    def _():
        issue(i + 1, (i + 1) % 2)

    half = i % 2

    def drain(k, carry):
        for slot in range(TOP_K):
            copy(i, half, k, slot).wait()
        return carry

    lax.fori_loop(0, t, drain, 0, unroll=DMA_LOOP_UNROLL)
    meta = meta_ref[...]
    f = meta[:, 2:3] * _rows_from_tiles(buf.at[half, 0]) + meta[:, 3:4] * _rows_from_tiles(buf.at[half, 1])
    o_ref[...] = _layer_norm(alpha * h_ref[...] + g2_ref[...] * f) * lng_ref[...] + lnb_ref[...]


def _combine(dest_flat, yout, meta, h1, mods, ln_g, ln_b, tps, alpha):
    n = h1.shape[0]
    t = COMBINE_TILE
    d = D_MODEL

    def mod_map(i, dest):
        return (jnp.where(i % tps == 0, 8, i // tps), 0, 5)

    return pl.pallas_call(
        functools.partial(_combine_kernel, alpha=alpha),
        grid_spec=pltpu.PrefetchScalarGridSpec(
            num_scalar_prefetch=1,
            grid=(n // t,),
            in_specs=[pl.BlockSpec(memory_space=pl.ANY),
                      pl.BlockSpec((t, LANES), lambda i, dest: (i, 0)),
                      pl.BlockSpec((t, d), lambda i, dest: (i, 0)),
                      pl.BlockSpec((None, 1, d), mod_map),
                      pl.BlockSpec((1, d), lambda i, dest: (0, 0)),
                      pl.BlockSpec((1, d), lambda i, dest: (0, 0))],
            out_specs=pl.BlockSpec((t, d), lambda i, dest: (i, 0)),
            scratch_shapes=[pltpu.VMEM((2, TOP_K, t * ROW_CHUNKS, LANES), F32), pltpu.SemaphoreType.DMA((2,))]),
        out_shape=jax.ShapeDtypeStruct((n, d), F32),
        compiler_params=_cparams(dimension_semantics=("arbitrary",)),
        name="moe_combine",
    )(dest_flat, yout, meta, h1, mods, ln_g, ln_b)


def _prep_w_in(w_in):
    cq, ckv, kr, z, xbc, dt, lx, lg, gate = _split_sections(w_in)
    zeros = lambda w: jnp.zeros(w_in.shape[:-1] + (w,), w_in.dtype)
    out = jnp.concatenate([z, lx, lg, gate, xbc, cq, ckv, kr, zeros(64),
                           dt[..., :SSD_HEADS], zeros(DT_DIR_STRIDE - SSD_HEADS),
                           dt[..., SSD_HEADS:], zeros(DT_DIR_STRIDE - SSD_HEADS)], axis=-1)
    return out.astype(BF16)


def _split_sections(w):
    sizes = (Q_LORA, KV_LORA, QK_ROPE, SSD_INNER, SSD_INNER + 2 * SSD_GROUPS * SSD_STATE, 2 * SSD_HEADS,
             LRU_WIDTH, LRU_WIDTH, N_BRANCH * D_MODEL)
    out, start = [], 0
    for size in sizes:
        out.append(w[..., start:start + size])
        start += size
    return out


def _dir_row(v):
    row = jnp.zeros((LANES,), F32)
    row = row.at[0:SSD_HEADS].set(v[0]).at[DT_DIR_STRIDE:DT_DIR_STRIDE + SSD_HEADS].set(v[1])
    return row[None, :]


def _expand_matrix(rev):
    rows = jnp.arange(LANES)[:, None]
    cols = jnp.arange(SSD_INNER)[None, :]
    base = DT_DIR_STRIDE if rev else 0
    return (rows - base == cols // SSD_HEADDIM).astype(BF16)


def _block_diag_gates(wa, wx):
    per = LRU_GROUP // LRU_BW
    eye = jnp.eye(per, dtype=wa.dtype)

    def bd(w):
        w = w.reshape(LRU_WIDTH // LRU_GROUP, per, LRU_BW, LRU_BW)
        return jnp.einsum('gicd,ij->gicjd', w, eye).reshape(LRU_WIDTH // LRU_GROUP, LRU_GROUP, LRU_GROUP)

    return jnp.concatenate([bd(wa), bd(wx)], axis=-1).astype(BF16)


def _rope_tables(n_ctx, seq):
    rows = seq // GRID_W
    row_pos = jnp.repeat(jnp.arange(rows, dtype=F32), GRID_W)
    col_pos = (jnp.arange(rows * GRID_W) % GRID_W).astype(F32)
    inv_freq = ROPE_THETA ** (-jnp.arange(ROPE_FREQS, dtype=F32) / ROPE_FREQS)
    ang = [row_pos[:, None] * inv_freq, col_pos[:, None] * inv_freq]
    cos = jnp.concatenate([jnp.cos(ang[0]), jnp.cos(ang[0]), jnp.cos(ang[1]), jnp.cos(ang[1])], axis=1)
    sin = jnp.concatenate([-jnp.sin(ang[0]), jnp.sin(ang[0]), -jnp.sin(ang[1]), jnp.sin(ang[1])], axis=1)
    pad = lambda t, fill: jnp.concatenate(
        [jnp.concatenate([jnp.full((n_ctx, QK_ROPE), fill, F32), t], axis=0),
         jnp.zeros((n_ctx + seq, LANES - QK_ROPE), F32)], axis=1)
    return pad(cos, 1.0), pad(sin, 0.0)


def kernel(x, c, ctx, c_ctx, w_mod, b_mod, w_in, q_norm_w, kv_norm_w, w_uq, w_ukv, ssd_conv_w, ssd_conv_b, ssd_a_log, ssd_dt_bias, ssd_d, ssd_norm_w, lru_conv_w, lru_conv_b, lru_wa, lru_ba, lru_wx, lru_bx, lru_lambda, w_branch, w_out, ln1_g, ln1_b, ln2_g, ln2_b, router_wg, router_bg, router_we, router_be, exp_w1, exp_w3, exp_w2):
    bsz, seq, d = x.shape
    n_ctx = ctx.shape[1]
    depth = w_mod.shape[0]
    assert d == D_MODEL and n_ctx == ROW_TILE and seq % ROW_TILE == 0 and bsz == SUBLANES
    s_all = n_ctx + seq
    tps = s_all // ROW_TILE
    n = bsz * s_all
    alpha = (2 * depth) ** 0.25

    h = jnp.concatenate([ctx, x], axis=1).reshape(n, d)
    cvec = jnp.zeros((16, d), F32).at[:bsz].set(c).at[bsz].set(c_ctx)
    mods_all = _mod_vectors(cvec, w_mod, b_mod).reshape(depth, 16, 1, 6 * d)
    cos_t, sin_t = _rope_tables(n_ctx, seq)
    expand = (_expand_matrix(False), _expand_matrix(True))

    w_in_p = _prep_w_in(w_in)
    wq = jnp.pad(w_uq.reshape(depth, Q_LORA, MLA_HEADS, QK_NOPE + QK_ROPE),
                 ((0, 0), (0, 0), (0, 0), (0, QK_PAD - QK_NOPE - QK_ROPE))).reshape(depth, Q_LORA, -1).astype(BF16)
    wkv4 = w_ukv.reshape(depth, KV_LORA, MLA_HEADS, QK_NOPE + V_DIM)
    wkv = jnp.concatenate([wkv4[..., :QK_NOPE].reshape(depth, KV_LORA, -1),
                           wkv4[..., QK_NOPE:].reshape(depth, KV_LORA, -1)], axis=-1).astype(BF16)
    w_router = jnp.concatenate([router_wg, router_we,
                                jnp.zeros((depth, d, LANES - N_GROUPS - N_EXPERTS), F32)], axis=-1)
    b_router = jnp.concatenate([router_bg, router_be,
                                jnp.zeros((depth, LANES - N_GROUPS - N_EXPERTS), F32)], axis=-1)
    wb_all = w_branch.astype(BF16)
    wo_all = w_out.astype(BF16)

    n_blocks = -(-(n * TOP_K) // EXPERT_BLOCK) + N_EXPERTS
    row1 = lambda v: v.reshape(1, -1)

    for l in range(depth):
        mods = mods_all[l]
        proj, misc = _in_proj(h, mods, w_in_p, l, tps)
        proj3 = proj.reshape(bsz, s_all, -1)
        misc3 = misc.reshape(bsz, s_all, -1)

        q, k, vt = _mla_prep(proj, misc, row1(q_norm_w[l]), row1(kv_norm_w[l]), wq, wkv, l, cos_t, sin_t, tps)
        att = _attention(q.reshape(bsz, s_all, -1), k.reshape(bsz, s_all, -1), vt, n_ctx)

        xbc_conv = _dwconv(proj3, COL_XBC, 2 * SSD_INNER, ssd_conv_w[l], row1(ssd_conv_b[l]), True)
        dtb = _dir_row(ssd_dt_bias[l])
        aneg = _dir_row(-jnp.exp(ssd_a_log[l]))
        ys = _ssd_scan(xbc_conv, misc3, dtb, aneg, expand, n_ctx)

        lx_conv = _dwconv(proj3, COL_LX, LRU_WIDTH, lru_conv_w[l], row1(lru_conv_b[l]), False)
        hs = [_lru_scan(lx_conv, _block_diag_gates(lru_wa[l, dr], lru_wx[l, dr]), row1(lru_ba[l, dr]),
                        row1(lru_bx[l, dr]), row1(lru_lambda[l, dr]), n_ctx, bool(dr)) for dr in range(2)]

        d_skip_row = row1(jnp.repeat(ssd_d[l], SSD_HEADDIM))
        h1, u2t = _merge(att.reshape(n, -1), ys[0].reshape(n, -1), ys[1].reshape(n, -1),
                         xbc_conv.reshape(n, -1), proj, hs[0].reshape(n, -1), hs[1].reshape(n, -1), h,
                         d_skip_row, row1(ssd_norm_w[l]), wb_all, wo_all, l, mods,
                         row1(ln1_g[l]), row1(ln1_b[l]), tps, alpha)

        meta, meta_t, cnt = _router(u2t, w_router[l], row1(b_router[l]))
        counts = cnt[0, :N_EXPERTS].astype(jnp.int32)
        padded = (counts + EXPERT_BLOCK - 1) // EXPERT_BLOCK * EXPERT_BLOCK
        pend = jnp.cumsum(padded)
        pstart = pend - padded
        eid = meta_t[0:TOP_K].astype(jnp.int32)
        rank = meta_t[4:4 + TOP_K].astype(jnp.int32)
        dest = (jnp.take(pstart, eid) + rank).reshape(-1)
        blk_start = jnp.arange(n_blocks, dtype=jnp.int32) * EXPERT_BLOCK
        blk_e = jnp.minimum(jnp.sum(pend[None, :] <= blk_start[:, None], axis=1), N_EXPERTS - 1).astype(jnp.int32)
        nact = (pend[-1:] // EXPERT_BLOCK).astype(jnp.int32)
        blk_e = jnp.where(jnp.arange(n_blocks) < nact[0], blk_e, jnp.take(blk_e, jnp.maximum(nact[0] - 1, 0)))

        xin = _dispatch(dest, pend.astype(jnp.int32), u2t, n_blocks * EXPERT_BLOCK)
        yout = _experts(blk_e, nact, xin, exp_w1, exp_w3, exp_w2, l)
        h = _combine(dest, yout, meta, h1, mods, row1(ln2_g[l]), row1(ln2_b[l]), tps, alpha)

    return h.reshape(bsz, s_all, d)[:, n_ctx:, :]
```

```python
import functools
import math

import jax
import jax.numpy as jnp
from jax import lax
from jax.experimental import pallas as pl
from jax.experimental.pallas import tpu as pltpu

F32 = jnp.float32
BF16 = jnp.bfloat16

D_MODEL = 1024
GRID_W = 64
N_BRANCH = 3

MLA_HEADS = 8
QK_NOPE = 128
QK_ROPE = 64
V_DIM = 128
Q_LORA = 512
KV_LORA = 256
ROPE_FREQS = QK_ROPE // 4
ROPE_THETA = 10000.0
ATTN_SCALE = (QK_NOPE + QK_ROPE) ** -0.5
QK_PAD = 256

SSD_HEADDIM = 64
SSD_INNER = 1024
SSD_HEADS = 16
SSD_GROUPS = 4
SSD_STATE = 128
SSD_CHUNK = 128
GROUP_W = SSD_INNER // SSD_GROUPS

CONV_W = 4
LRU_WIDTH = 1024
LRU_BW = 64
LRU_C = 8.0
LRU_TILE = 128
LRU_GROUP = 256

N_GROUPS = 4
EXPERTS_PER_GROUP = 8
N_EXPERTS = 32
TOP_K = 2
EXPERT_HIDDEN = 512
EXPERT_BLOCK = 256

LN_EPS = 1e-5
RMS_EPS = 1e-6

LANES = 128
SUBLANES = 8
ROW_TILE = 256
ROW_CHUNKS = D_MODEL // LANES
VMEM_LIMIT = 56 * 1024 * 1024

COL_Z = 0
COL_LX = 1024
COL_LG = 2048
COL_GATE = 3072
COL_XBC = 6144
COL_CQ = 8192
COL_CKV = 8704
COL_KR = 8960
COL_DT = 9088
DT_DIR_STRIDE = 64
IN_COLS_PAD = 9216


def _cparams(**kw):
    return pltpu.CompilerParams(vmem_limit_bytes=VMEM_LIMIT, **kw)


def _split3(x):
    hi = x.astype(BF16)
    r = x - hi.astype(F32)
    mid = r.astype(BF16)
    lo = (r - mid.astype(F32)).astype(BF16)
    return hi, mid, lo


def _dot(a, b):
    return jnp.dot(a, b, preferred_element_type=F32)


def _dot_nt(a, b):
    return lax.dot_general(a, b, (((1,), (1,)), ((), ())), preferred_element_type=F32)


def _sigmoid(x):
    return 1.0 / (1.0 + jnp.exp(-x))


def _sigmoid_tanh(x):
    return 0.5 * jnp.tanh(0.5 * x) + 0.5


def _silu(x):
    return x * _sigmoid(x)


def _softplus(x):
    return jnp.maximum(x, 0.0) + jnp.log(1.0 + jnp.exp(-jnp.abs(x)))


def _layer_norm(t):
    mu = jnp.mean(t, axis=-1, keepdims=True)
    c = t - mu
    var = jnp.mean(c * c, axis=-1, keepdims=True)
    return c * lax.rsqrt(var + LN_EPS)


def _mod_kernel(c_ref, w_ref, b_ref, o_ref):
    s = _silu(c_ref[...]).astype(BF16)
    o_ref[...] = _dot(s, w_ref[...].astype(BF16)) + b_ref[...]


def _mod_vectors(cvec, w_mod, b_mod):
    depth, d, n = w_mod.shape
    tn = 1536
    return pl.pallas_call(
        _mod_kernel,
        grid=(depth, n // tn),
        in_specs=[pl.BlockSpec((16, d), lambda l, j: (0, 0)),
                  pl.BlockSpec((None, d, tn), lambda l, j: (l, 0, j)),
                  pl.BlockSpec((None, 1, tn), lambda l, j: (l, 0, j))],
        out_specs=pl.BlockSpec((None, 16, tn), lambda l, j: (l, 0, j)),
        out_shape=jax.ShapeDtypeStruct((depth, 16, n), F32),
        compiler_params=_cparams(),
        name="mod_vectors",
    )(cvec, w_mod, b_mod.reshape(depth, 1, n))


def _mod_spec(tiles_per_sample, k):
    def imap(i):
        return (jnp.where(i % tiles_per_sample == 0, 8, i // tiles_per_sample), 0, k)
    return pl.BlockSpec((None, 1, D_MODEL), imap)


IN_PROJ_COLS = 1024


def _in_proj_kernel(h_ref, sh_ref, sc_ref, w_ref, o_ref, misc_ref):
    u = (_layer_norm(h_ref[...]) * (1.0 + sc_ref[...]) + sh_ref[...]).astype(BF16)
    main = o_ref.shape[1]
    for c0 in range(0, main, IN_PROJ_COLS):
        cs = slice(c0, min(c0 + IN_PROJ_COLS, main))
        o_ref[:, cs] = _dot(u, w_ref[:, cs]).astype(o_ref.dtype)
    misc_ref[...] = _dot(u, w_ref[:, main:])


def _in_proj(h, mods, w_all, layer, tps):
    n = h.shape[0]
    _, k, cols = w_all.shape
    return pl.pallas_call(
        _in_proj_kernel,
        grid=(n // ROW_TILE,),
        in_specs=[pl.BlockSpec((ROW_TILE, k), lambda i: (i, 0)),
                  _mod_spec(tps, 0), _mod_spec(tps, 1),
                  pl.BlockSpec((None, k, cols), lambda i: (layer, 0, 0), pipeline_mode=pl.Buffered(1))],
        out_specs=[pl.BlockSpec((ROW_TILE, COL_KR), lambda i: (i, 0)),
                   pl.BlockSpec((ROW_TILE, cols - COL_KR), lambda i: (i, 0))],
        out_shape=[jax.ShapeDtypeStruct((n, COL_KR), BF16),
                   jax.ShapeDtypeStruct((n, cols - COL_KR), F32)],
        compiler_params=_cparams(),
        name="in_proj",
    )(h, mods, mods, w_all)


def _rope128(t, cos, sin):
    lane = lax.broadcasted_iota(jnp.int32, t.shape, 1)
    partner = jnp.where(lane % (2 * ROPE_FREQS) < ROPE_FREQS,
                        pltpu.roll(t, LANES - ROPE_FREQS, 1), pltpu.roll(t, ROPE_FREQS, 1))
    return t * cos + partner * sin


def _rms(t, gain):
    return t * lax.rsqrt(jnp.mean(t * t, axis=-1, keepdims=True) + RMS_EPS) * gain


def _mla_prep_kernel(cq_ref, ckv_ref, kr_ref, qg_ref, kvg_ref, wq_ref, wkv_ref, cos_ref, sin_ref,
                     q_ref, k_ref, vt_ref):
    cos = cos_ref[...]
    sin = sin_ref[...]
    q = _dot(_rms(cq_ref[...].astype(F32), qg_ref[...]).astype(BF16), wq_ref[...]) * ATTN_SCALE
    kv = _dot(_rms(ckv_ref[...].astype(F32), kvg_ref[...]).astype(BF16), wkv_ref[...])
    krz = _rope128(kr_ref[...], cos, sin).astype(BF16)
    for h in range(MLA_HEADS):
        c0 = h * QK_PAD
        q_ref[:, c0:c0 + QK_NOPE] = q[:, c0:c0 + QK_NOPE].astype(BF16)
        q_ref[:, c0 + QK_NOPE:c0 + QK_PAD] = _rope128(q[:, c0 + QK_NOPE:c0 + QK_PAD], cos, sin).astype(BF16)
        k_ref[:, c0:c0 + QK_NOPE] = kv[:, h * QK_NOPE:(h + 1) * QK_NOPE].astype(BF16)
        k_ref[:, c0 + QK_NOPE:c0 + QK_PAD] = krz
    vt_ref[...] = kv[:, MLA_HEADS * QK_NOPE:].T.astype(BF16)


def _mla_prep(proj, misc, q_gain, kv_gain, wq_all, wkv_all, layer, cos_t, sin_t, tps):
    n = proj.shape[0]
    t = ROW_TILE
    row = lambda w, cb: pl.BlockSpec((t, w), lambda i: (i, cb))
    const = lambda shape: pl.BlockSpec(shape, lambda i: (0, 0))
    stacked = lambda shape: pl.BlockSpec((None,) + shape, lambda i: (layer, 0, 0))
    return pl.pallas_call(
        _mla_prep_kernel,
        grid=(n // t,),
        in_specs=[row(Q_LORA, COL_CQ // Q_LORA), row(KV_LORA, COL_CKV // KV_LORA), row(LANES, 0),
                  const((1, Q_LORA)), const((1, KV_LORA)),
                  stacked((Q_LORA, MLA_HEADS * QK_PAD)), stacked((KV_LORA, 2 * MLA_HEADS * QK_NOPE)),
                  pl.BlockSpec((t, LANES), lambda i: (i % tps, 0)),
                  pl.BlockSpec((t, LANES), lambda i: (i % tps, 0))],
        out_specs=[pl.BlockSpec((t, MLA_HEADS * QK_PAD), lambda i: (i, 0)),
                   pl.BlockSpec((t, MLA_HEADS * QK_PAD), lambda i: (i, 0)),
                   pl.BlockSpec((None, MLA_HEADS * V_DIM, t), lambda i: (i // tps, 0, i % tps))],
        out_shape=[jax.ShapeDtypeStruct((n, MLA_HEADS * QK_PAD), BF16),
                   jax.ShapeDtypeStruct((n, MLA_HEADS * QK_PAD), BF16),
                   jax.ShapeDtypeStruct((n // (tps * t), MLA_HEADS * V_DIM, tps * t), BF16)],
        compiler_params=_cparams(),
        name="mla_prep",
    )(proj, proj, misc, q_gain, kv_gain, wq_all, wkv_all, cos_t, sin_t)


ATTN_KEY_CHUNK = 1152


def _attn_kernel(q_ref, k_ref, vt_ref, o_ref, s0_ref, s1_ref, *, n_ctx):
    s_all = k_ref.shape[0]
    tq = ROW_TILE
    n_tiles = (s_all - n_ctx) // tq
    chunks = [(c, min(c + ATTN_KEY_CHUNK, s_all)) for c in range(0, s_all, ATTN_KEY_CHUNK)]

    def scores(s_ref, r0, key_chunks):
        q = q_ref[pl.ds(r0, tq), :]
        for c0, c1 in key_chunks:
            s_ref[c0:c1, :] = _dot_nt(k_ref[c0:c1, :], q)

    def finish(s_ref, r0, key_chunks):
        m = None
        for c0, c1 in key_chunks:
            cm = jnp.max(s_ref[c0:c1, :], axis=0, keepdims=True)
            m = cm if m is None else jnp.maximum(m, cm)
        l = acc = None
        for c0, c1 in key_chunks:
            p = jnp.exp(s_ref[c0:c1, :] - m)
            ps = jnp.sum(p, axis=0, keepdims=True)
            pv = _dot(vt_ref[:, c0:c1], p.astype(BF16))
            l, acc = (ps, pv) if l is None else (l + ps, acc + pv)
        o_ref[pl.ds(r0, tq), :] = (acc / l).T.astype(o_ref.dtype)

    scores(s0_ref, 0, [(0, n_ctx)])
    finish(s0_ref, 0, [(0, n_ctx)])
    scores(s0_ref, n_ctx, chunks)

    def body(j, carry):
        r_a = pl.multiple_of(n_ctx + 2 * j * tq, tq)
        r_b = pl.multiple_of(r_a + tq, tq)
        r_c = pl.multiple_of(jnp.minimum(r_b + tq, s_all - tq), tq)
        scores(s1_ref, r_b, chunks)
        finish(s0_ref, r_a, chunks)
        scores(s0_ref, r_c, chunks)
        finish(s1_ref, r_b, chunks)
        return carry

    assert n_tiles % 2 == 0
    lax.fori_loop(0, n_tiles // 2, body, 0)


def _attention(q, k, vt, n_ctx):
    b, s, _ = q.shape
    return pl.pallas_call(
        functools.partial(_attn_kernel, n_ctx=n_ctx),
        grid=(b, MLA_HEADS),
        in_specs=[pl.BlockSpec((None, s, QK_PAD), lambda bi, h: (bi, 0, h)),
                  pl.BlockSpec((None, s, QK_PAD), lambda bi, h: (bi, 0, h)),
                  pl.BlockSpec((None, V_DIM, s), lambda bi, h: (bi, h, 0))],
        out_specs=pl.BlockSpec((None, s, V_DIM), lambda bi, h: (bi, 0, h)),
        out_shape=jax.ShapeDtypeStruct((b, s, MLA_HEADS * V_DIM), BF16),
        scratch_shapes=[pltpu.VMEM((s, ROW_TILE), F32), pltpu.VMEM((s, ROW_TILE), F32)],
        compiler_params=_cparams(),
        name="attention",
    )(q, k, vt)


def _conv_kernel(x_ref, p_ref, n_ref, w_ref, b_ref, o_ref, *, silu, tps):
    j = pl.program_id(1)
    has_prev = (j > 1).astype(F32)
    has_next = jnp.logical_and(j > 0, j < tps - 1).astype(F32)
    x = x_ref[...].astype(F32)
    xe = jnp.concatenate([p_ref[...].astype(F32) * has_prev, x, n_ref[...].astype(F32) * has_next], axis=0)
    rows = xe.shape[0]
    t = x_ref.shape[0]
    w = w_ref[...]
    y = b_ref[...] + w[2:3, :] * x
    y = y + w[0:1, :] * pltpu.roll(xe, 2, 0)[CONV_HALO:CONV_HALO + t]
    y = y + w[1:2, :] * pltpu.roll(xe, 1, 0)[CONV_HALO:CONV_HALO + t]
    y = y + w[3:4, :] * pltpu.roll(xe, rows - 1, 0)[CONV_HALO:CONV_HALO + t]
    o_ref[...] = (_silu(y) if silu else y).astype(o_ref.dtype)


CONV_HALO = 16


def _dwconv(proj3, col0, width, w, bias, silu):
    b, s, _ = proj3.shape
    t = ROW_TILE
    tc = 1024
    cb = col0 // tc
    tps = s // t
    hb = t // CONV_HALO
    last = s // CONV_HALO - 1
    return pl.pallas_call(
        functools.partial(_conv_kernel, silu=silu, tps=tps),
        grid=(b, tps, width // tc),
        in_specs=[pl.BlockSpec((None, t, tc), lambda bi, j, c: (bi, j, cb + c)),
                  pl.BlockSpec((None, CONV_HALO, tc), lambda bi, j, c: (bi, jnp.maximum(j * hb - 1, 0), cb + c)),
                  pl.BlockSpec((None, CONV_HALO, tc), lambda bi, j, c: (bi, jnp.minimum((j + 1) * hb, last), cb + c)),
                  pl.BlockSpec((CONV_W, tc), lambda bi, j, c: (0, c)),
                  pl.BlockSpec((1, tc), lambda bi, j, c: (0, c))],
        out_specs=pl.BlockSpec((None, t, tc), lambda bi, j, c: (bi, j, c)),
        out_shape=jax.ShapeDtypeStruct((b, s, width), BF16),
        compiler_params=_cparams(),
        name="dwconv",
    )(proj3, proj3, proj3, w, bias)


def _ssd_kernel(xs0, b0, c0, dt0, xs1, b1, c1, dt1, dtb_ref, aneg_ref, e0_ref, e1_ref, y0_ref, y1_ref,
                h0_ref, h1_ref):
    @pl.when(pl.program_id(1) == 0)
    def _():
        h0_ref[...] = jnp.zeros_like(h0_ref)
        h1_ref[...] = jnp.zeros_like(h1_ref)

    _ssd_chunk(xs0, b0, c0, dt0, dtb_ref, aneg_ref, e0_ref, y0_ref, h0_ref, rev=False)
    _ssd_chunk(xs1, b1, c1, dt1, dtb_ref, aneg_ref, e1_ref, y1_ref, h1_ref, rev=True)


def _ssd_chunk(xs_ref, b_ref, c_ref, dt_ref, dtb_ref, aneg_ref, e_ref, y_ref, h_ref, *, rev):
    q = SSD_CHUNK
    dt = _softplus(dt_ref[...] + dtb_ref[...])
    a = dt * aneg_ref[...]
    row = lax.broadcasted_iota(jnp.int32, (q, q), 0)
    col = lax.broadcasted_iota(jnp.int32, (q, q), 1)
    tri = (col >= row) if rev else (col <= row)
    tri_b = jnp.where(tri, 1.0, 0.0).astype(BF16)
    a3 = _split3(a)
    acum = _dot(tri_b, a3[0]) + _dot(tri_b, a3[1]) + _dot(tri_b, a3[2])
    total = jnp.sum(a, axis=0, keepdims=True)
    stack = jnp.concatenate([dt, total - acum, acum, jnp.broadcast_to(total, (SUBLANES, LANES))], axis=0)
    e = e_ref[...]
    s3 = _split3(stack)
    ex = _dot(s3[0], e) + _dot(s3[1], e) + _dot(s3[2], e)
    dt_e = ex[0:q]
    to_end_e = jnp.exp(ex[q:2 * q])
    from_start_e = jnp.exp(ex[2 * q:3 * q])
    chunk_decay_e = jnp.exp(ex[3 * q:3 * q + 1])
    xg = xs_ref[...].astype(F32) * dt_e
    xg_b = xg.astype(BF16)
    w_b = (xg * to_end_e).astype(BF16)
    acum_t = acum.T
    base = DT_DIR_STRIDE if rev else 0
    for g in range(SSD_GROUPS):
        gs = slice(g * GROUP_W, (g + 1) * GROUP_W)
        bg = b_ref[:, g * SSD_STATE:(g + 1) * SSD_STATE]
        cg = c_ref[:, g * SSD_STATE:(g + 1) * SSD_STATE]
        cb = _dot_nt(cg, bg)
        h_prev = h_ref[:, gs]
        y_off = _dot(cg, h_prev.astype(BF16)) * from_start_e[:, gs]
        h_ref[:, gs] = chunk_decay_e[:, gs] * h_prev + _dot(bg.astype(F32).T.astype(BF16), w_b[:, gs])
        parts = []
        for hh in range(SSD_HEADS // SSD_GROUPS):
            head = g * (SSD_HEADS // SSD_GROUPS) + hh
            c = base + head
            seg = acum[:, c:c + 1] - acum_t[c:c + 1, :]
            lmat = jnp.exp(jnp.where(tri, seg, -1e30))
            parts.append(_dot((cb * lmat).astype(BF16), xg_b[:, head * SSD_HEADDIM:(head + 1) * SSD_HEADDIM]))
        y_ref[:, gs] = (jnp.concatenate(parts, axis=1) + y_off).astype(y_ref.dtype)


def _ssd_order(step, n_ctx_chunks, n_chunks, rev):
    if not rev:
        return step
    return jnp.where(step < n_ctx_chunks, n_ctx_chunks - 1 - step, n_chunks - 1 + n_ctx_chunks - step)


def _ssd_scan(xbc_conv, misc3, dt_bias_row, a_neg_row, expand, n_ctx):
    b, s, _ = xbc_conv.shape
    q = SSD_CHUNK
    nchunks = s // q
    const = lambda shape: pl.BlockSpec(shape, lambda bi, i: (0, 0))
    specs, out_specs = [], []
    for rev in (False, True):
        order = functools.partial(_ssd_order, n_ctx_chunks=n_ctx // q, n_chunks=nchunks, rev=rev)
        chunk = lambda w, cb, order=order: pl.BlockSpec((None, q, w), lambda bi, i: (bi, order(i), cb))
        specs += [chunk(SSD_INNER, 0), chunk(SSD_GROUPS * SSD_STATE, 2), chunk(SSD_GROUPS * SSD_STATE, 3),
                  chunk(LANES, (COL_DT - COL_KR) // LANES)]
        out_specs.append(chunk(SSD_INNER, 0))
    return pl.pallas_call(
        _ssd_kernel,
        grid=(b, nchunks),
        in_specs=specs + [const((1, LANES)), const((1, LANES)), const((LANES, SSD_INNER)), const((LANES, SSD_INNER))],
        out_specs=out_specs,
        out_shape=[jax.ShapeDtypeStruct((b, s, SSD_INNER), BF16)] * 2,
        scratch_shapes=[pltpu.VMEM((SSD_STATE, SSD_INNER), F32)] * 2,
        compiler_params=_cparams(),
        name="ssd_scan",
    )(xbc_conv, xbc_conv, xbc_conv, misc3, xbc_conv, xbc_conv, xbc_conv, misc3,
      dt_bias_row, a_neg_row, expand[0], expand[1])


def _lru_kernel(x_ref, w_ref, ba_ref, bx_ref, lam_ref, o_ref, a_s, b_s, h_s, carry, *, rev):
    nb, t, _ = x_ref.shape
    per = LRU_GROUP // LANES

    @pl.when(pl.program_id(0) == 0)
    def _():
        carry[...] = jnp.zeros_like(carry)

    decay = -LRU_C * _softplus(-lam_ref[...])
    for bi in range(nb):
        for g in range(LRU_WIDTH // LRU_GROUP):
            gs = slice(g * LRU_GROUP, (g + 1) * LRU_GROUP)
            xd = x_ref[bi, :, gs]
            ri = _dot(xd, w_ref[g])
            r = _sigmoid_tanh(ri[:, :LRU_GROUP] + ba_ref[:, gs])
            i = _sigmoid_tanh(ri[:, LRU_GROUP:] + bx_ref[:, gs])
            log_a = decay[:, gs] * r
            a = jnp.exp(log_a)
            bt = jnp.sqrt(jnp.tanh(-log_a) * (1.0 + a * a)) * (i * xd.astype(F32))
            for cc in range(per):
                ls = slice(cc * LANES, (cc + 1) * LANES)
                a_s[g * per + cc, pl.ds(bi, t, stride=nb), :] = a[:, ls]
                b_s[g * per + cc, pl.ds(bi, t, stride=nb), :] = bt[:, ls]

    def step(k, h):
        tt = (t - 1 - k) if rev else k
        r0 = pl.multiple_of(tt * nb, nb)
        h = a_s[:, pl.ds(r0, nb), :] * h + b_s[:, pl.ds(r0, nb), :]
        h_s[:, pl.ds(r0, nb), :] = h
        return h

    carry[...] = lax.fori_loop(0, t, step, carry[...])
    for bi in range(nb):
        for cc in range(LRU_WIDTH // LANES):
            o_ref[bi, :, cc * LANES:(cc + 1) * LANES] = h_s[cc, pl.ds(bi, t, stride=nb), :].astype(o_ref.dtype)


def _lru_scan(lx_conv, w_gate, ba, bx, lam, n_ctx, rev):
    b, s, width = lx_conv.shape
    t = LRU_TILE
    ntiles = s // t
    order = functools.partial(_ssd_order, n_ctx_chunks=n_ctx // t, n_chunks=ntiles, rev=rev)
    const = lambda shape: pl.BlockSpec(shape, lambda i: (0,) * len(shape))
    return pl.pallas_call(
        functools.partial(_lru_kernel, rev=rev),
        grid=(ntiles,),
        in_specs=[pl.BlockSpec((b, t, width), lambda i: (0, order(i), 0)),
                  const(w_gate.shape), const((1, width)), const((1, width)), const((1, width))],
        out_specs=pl.BlockSpec((b, t, width), lambda i: (0, order(i), 0)),
        out_shape=jax.ShapeDtypeStruct((b, s, width), BF16),
        scratch_shapes=[pltpu.VMEM((width // LANES, t * b, LANES), F32)] * 3
        + [pltpu.VMEM((width // LANES, b, LANES), F32)],
        compiler_params=_cparams(),
        name="lru_rev" if rev else "lru_fwd",
    )(lx_conv, w_gate, ba, bx, lam)


def _gelu_tanh(x):
    return 0.5 * x * (1.0 + jnp.tanh(math.sqrt(2.0 / math.pi) * (x + 0.044715 * (x * x * x))))


def _merge_kernel(att_ref, y0_ref, y1_ref, xs_ref, z_ref, l0_ref, l1_ref, lg_ref, gate_ref, h_ref,
                  dskip_ref, ssdg_ref, wb_ref, wo_ref, g1_ref, lng_ref, lnb_ref, sh2_ref, sc2_ref,
                  h1_ref, u2_ref, *, alpha):
    f32 = lambda ref: ref[...].astype(F32)
    y = f32(y0_ref) + f32(y1_ref) + f32(xs_ref) * dskip_ref[...]
    z = f32(z_ref)
    y = y * (z * _sigmoid_tanh(z))
    parts = []
    for g in range(SSD_GROUPS):
        yg = y[:, g * GROUP_W:(g + 1) * GROUP_W]
        parts.append(yg * lax.rsqrt(jnp.mean(yg * yg, axis=-1, keepdims=True) + RMS_EPS))
    y_ssd = jnp.concatenate(parts, axis=1) * ssdg_ref[...]
    y_lru = (f32(l0_ref) + f32(l1_ref)) * _gelu_tanh(f32(lg_ref))
    branches = (att_ref[...], y_ssd.astype(BF16), y_lru.astype(BF16))
    mix = None
    for k in range(N_BRANCH):
        gate = gate_ref[:, k * D_MODEL:(k + 1) * D_MODEL].astype(F32)
        term = _sigmoid_tanh(gate) * _dot(branches[k], wb_ref[k])
        mix = term if mix is None else mix + term
    out = _dot(mix.astype(BF16), wo_ref[...])
    h1 = _layer_norm(alpha * h_ref[...] + g1_ref[...] * out) * lng_ref[...] + lnb_ref[...]
    h1_ref[...] = h1
    u2 = _layer_norm(h1) * (1.0 + sc2_ref[...]) + sh2_ref[...]
    for s in range(D_MODEL // LANES):
        u2_ref[pl.ds(s, u2.shape[0], stride=ROW_CHUNKS), :] = u2[:, s * LANES:(s + 1) * LANES]


def _merge(att, y0, y1, xbc_conv, proj, l0, l1, h, d_skip_row, ssd_gain, wb_all, wo_all, layer, mods, ln_g, ln_b,
           tps, alpha):
    n = h.shape[0]
    t = ROW_TILE
    d = D_MODEL
    row = lambda cb: pl.BlockSpec((t, d), lambda i: (i, cb))
    const = lambda shape: pl.BlockSpec(shape, lambda i: (0,) * len(shape))
    stacked = lambda shape: pl.BlockSpec((None,) + shape, lambda i: (layer,) + (0,) * len(shape))
    return pl.pallas_call(
        functools.partial(_merge_kernel, alpha=alpha),
        grid=(n // t,),
        in_specs=[row(0), row(0), row(0), row(0), row(COL_Z // d), row(0), row(0), row(COL_LG // d),
                  pl.BlockSpec((t, N_BRANCH * d), lambda i: (i, COL_GATE // (N_BRANCH * d))), row(0),
                  const((1, d)), const((1, d)), stacked(wb_all.shape[1:]), stacked(wo_all.shape[1:]),
                  _mod_spec(tps, 2), const((1, d)), const((1, d)), _mod_spec(tps, 3), _mod_spec(tps, 4)],
        out_specs=[row(0), pl.BlockSpec((t * ROW_CHUNKS, LANES), lambda i: (i, 0))],
        out_shape=[jax.ShapeDtypeStruct((n, d), F32),
                   jax.ShapeDtypeStruct((n * ROW_CHUNKS, LANES), F32)],
        compiler_params=_cparams(),
        name="merge",
    )(att, y0, y1, xbc_conv, proj, l0, l1, proj, proj, h, d_skip_row, ssd_gain, wb_all, wo_all,
      mods, ln_g, ln_b, mods, mods)


def _row_tile(ref, r):
    return ref.at[pl.ds(pl.multiple_of(r * ROW_CHUNKS, ROW_CHUNKS), ROW_CHUNKS), :]


def _rows_from_tiles(ref):
    rows = ref.shape[0] // ROW_CHUNKS
    return jnp.concatenate([ref[pl.ds(s, rows, stride=ROW_CHUNKS), :] for s in range(ROW_CHUNKS)], axis=1)


def _router_kernel(u_ref, w_ref, b_ref, meta_ref, meta_t_ref, cnt_ref, count):
    @pl.when(pl.program_id(0) == 0)
    def _():
        count[...] = jnp.zeros_like(count)

    u = _rows_from_tiles(u_ref)
    t = u.shape[0]
    uh = u.astype(BF16)
    ul = (u - uh.astype(F32)).astype(BF16)
    w = w_ref[...]
    wh = w.astype(BF16)
    wl = (w - wh.astype(F32)).astype(BF16)
    logits = _dot(uh, wh) + _dot(uh, wl) + _dot(ul, wh) + b_ref[...]
    lane = lax.broadcasted_iota(jnp.int32, logits.shape, 1)
    neg = -jnp.inf
    big = 4 * LANES

    def top1(vals):
        m = jnp.max(vals, axis=-1, keepdims=True)
        idx = jnp.min(jnp.where(vals == m, lane, big), axis=-1, keepdims=True)
        return m, idx

    glog = jnp.where(lane < N_GROUPS, logits, neg)
    gmax, gsel = top1(glog)
    gval = 1.0 / jnp.sum(jnp.exp(glog - gmax), axis=-1, keepdims=True)
    lo = N_GROUPS + gsel * EXPERTS_PER_GROUP
    elog = jnp.where(jnp.logical_and(lane >= lo, lane < lo + EXPERTS_PER_GROUP), logits, neg)
    v1, i1 = top1(elog)
    v2, i2 = top1(jnp.where(lane == i1, neg, elog))
    e21 = jnp.exp(v2 - v1)
    w1 = gval / (1.0 + e21)
    w2 = gval * e21 / (1.0 + e21)
    e1 = i1 - N_GROUPS
    e2 = i2 - N_GROUPS
    onehot = jnp.logical_or(lane == e1, lane == e2)
    oh = jnp.where(onehot, 1.0, 0.0)
    r = lax.broadcasted_iota(jnp.int32, (t, t), 0)
    c = lax.broadcasted_iota(jnp.int32, (t, t), 1)
    before = jnp.where(c < r, 1.0, 0.0).astype(BF16)
    prefix = _dot(before, oh.astype(BF16)) + count[0:1, :]
    rank1 = jnp.sum(jnp.where(lane == e1, prefix, 0.0), axis=-1, keepdims=True)
    rank2 = jnp.sum(jnp.where(lane == e2, prefix, 0.0), axis=-1, keepdims=True)
    meta = jnp.where(lane == 0, e1.astype(F32), 0.0)
    meta = jnp.where(lane == 1, e2.astype(F32), meta)
    meta = jnp.where(lane == 2, w1, meta)
    meta = jnp.where(lane == 3, w2, meta)
    meta = jnp.where(lane == 4, rank1, meta)
    meta = jnp.where(lane == 5, rank2, meta)
    meta_ref[...] = meta
    meta_t_ref[...] = meta.T[0:SUBLANES, :]
    count[...] = count[...] + jnp.sum(oh, axis=0, keepdims=True)
    cnt_ref[...] = count[...]


def _router(u2t, w_router, b_router):
    n = u2t.shape[0] // ROW_CHUNKS
    t = ROW_TILE
    return pl.pallas_call(
        _router_kernel,
        grid=(n // t,),
        in_specs=[pl.BlockSpec((t * ROW_CHUNKS, LANES), lambda i: (i, 0)),
                  pl.BlockSpec((D_MODEL, LANES), lambda i: (0, 0)),
                  pl.BlockSpec((1, LANES), lambda i: (0, 0))],
        out_specs=[pl.BlockSpec((t, LANES), lambda i: (i, 0)),
                   pl.BlockSpec((SUBLANES, t), lambda i: (0, i)),
                   pl.BlockSpec((SUBLANES, LANES), lambda i: (0, 0))],
        out_shape=[jax.ShapeDtypeStruct((n, LANES), F32),
                   jax.ShapeDtypeStruct((SUBLANES, n), F32),
                   jax.ShapeDtypeStruct((SUBLANES, LANES), F32)],
        scratch_shapes=[pltpu.VMEM((SUBLANES, LANES), F32)],
        compiler_params=_cparams(),
        name="router",
    )(u2t, w_router, b_router)


DISPATCH_TILE = 512
DMA_LOOP_UNROLL = 8


def _dispatch_kernel(dest_ref, pend_ref, u_ref, x_ref, zeros, sem, zsem):
    base = pl.program_id(0) * DISPATCH_TILE

    @pl.when(pl.program_id(0) == 0)
    def _():
        zeros[...] = jnp.zeros_like(zeros)

        def tail(e):
            start = pl.multiple_of((pend_ref[e] - EXPERT_BLOCK) * ROW_CHUNKS, EXPERT_BLOCK * ROW_CHUNKS)
            return pltpu.make_async_copy(zeros, x_ref.at[pl.ds(start, EXPERT_BLOCK * ROW_CHUNKS), :], zsem)

        def nonempty(e):
            return pend_ref[e] > jnp.where(e == 0, 0, pend_ref[jnp.maximum(e - 1, 0)])

        def fill(e, carry):
            @pl.when(nonempty(e))
            def _():
                tail(e).start()
            return carry

        def done(e, carry):
            @pl.when(nonempty(e))
            def _():
                tail(e).wait()
            return carry

        lax.fori_loop(0, N_EXPERTS, fill, 0)
        lax.fori_loop(0, N_EXPERTS, done, 0)

        n_blocks = x_ref.shape[0] // (EXPERT_BLOCK * ROW_CHUNKS)
        n_active = pend_ref[N_EXPERTS - 1] // EXPERT_BLOCK

        def unused(b):
            start = pl.multiple_of(b * (EXPERT_BLOCK * ROW_CHUNKS), EXPERT_BLOCK * ROW_CHUNKS)
            return pltpu.make_async_copy(zeros, x_ref.at[pl.ds(start, EXPERT_BLOCK * ROW_CHUNKS), :], zsem)

        lax.fori_loop(n_active, n_blocks, lambda b, c: (unused(b).start(), c)[1], 0)
        lax.fori_loop(n_active, n_blocks, lambda b, c: (unused(b).wait(), c)[1], 0)

    def copy(k, slot):
        n_tok = dest_ref.shape[0] // TOP_K
        return pltpu.make_async_copy(_row_tile(u_ref, k), _row_tile(x_ref, dest_ref[slot * n_tok + base + k]), sem)

    def issue(k, carry):
        for slot in range(TOP_K):
            copy(k, slot).start(priority=slot)
        return carry

    def drain(k, carry):
        for slot in range(TOP_K):
            copy(k, slot).wait()
        return carry

    lax.fori_loop(0, DISPATCH_TILE, issue, 0, unroll=DMA_LOOP_UNROLL)
    lax.fori_loop(0, DISPATCH_TILE, drain, 0, unroll=DMA_LOOP_UNROLL)


def _dispatch(dest_flat, pend, u2t, n_rows):
    n = u2t.shape[0] // ROW_CHUNKS
    return pl.pallas_call(
        _dispatch_kernel,
        grid_spec=pltpu.PrefetchScalarGridSpec(
            num_scalar_prefetch=2,
            grid=(n // DISPATCH_TILE,),
            in_specs=[pl.BlockSpec((DISPATCH_TILE * ROW_CHUNKS, LANES), lambda i, dest, pend: (i, 0))],
            out_specs=pl.BlockSpec(memory_space=pl.ANY),
            scratch_shapes=[pltpu.VMEM((EXPERT_BLOCK * ROW_CHUNKS, LANES), u2t.dtype),
                            pltpu.SemaphoreType.DMA(()), pltpu.SemaphoreType.DMA(())]),
        out_shape=jax.ShapeDtypeStruct((n_rows * ROW_CHUNKS, LANES), u2t.dtype),
        compiler_params=_cparams(dimension_semantics=("arbitrary",)),
        name="moe_dispatch",
    )(dest_flat, pend, u2t)


def _expert_kernel(blk_ref, nact_ref, x_ref, w1_ref, w3_ref, w2_ref, y_ref, w1_b, w3_b, w2_b):
    i = pl.program_id(0)

    @pl.when(jnp.logical_or(i == 0, blk_ref[i] != blk_ref[jnp.maximum(i - 1, 0)]))
    def _():
        w1_b[...] = w1_ref[...].astype(BF16)
        w3_b[...] = w3_ref[...].astype(BF16)
        w2_b[...] = w2_ref[...].astype(BF16)

    @pl.when(i < nact_ref[0])
    def _():
        x = _rows_from_tiles(x_ref).astype(BF16)
        hid = _silu(_dot(x, w1_b[...])) * _dot(x, w3_b[...])
        y = _dot(hid.astype(BF16), w2_b[...])
        for s in range(ROW_CHUNKS):
            y_ref[pl.ds(s, EXPERT_BLOCK, stride=ROW_CHUNKS), :] = y[:, s * LANES:(s + 1) * LANES]

    @pl.when(i >= nact_ref[0])
    def _():
        y_ref[...] = jnp.zeros_like(y_ref)


def _experts(blk_e, nact, xin, w1_all, w3_all, w2_all, layer):
    nb = xin.shape[0] // (EXPERT_BLOCK * ROW_CHUNKS)
    tile = (EXPERT_BLOCK * ROW_CHUNKS, LANES)
    weight = lambda r, c: pl.BlockSpec((None, None, r, c), lambda i, blk, na: (layer, blk[i], 0, 0))
    return pl.pallas_call(
        _expert_kernel,
        grid_spec=pltpu.PrefetchScalarGridSpec(
            num_scalar_prefetch=2,
            grid=(nb,),
            in_specs=[pl.BlockSpec(tile, lambda i, blk, na: (jnp.minimum(i, na[0] - 1), 0)),
                      weight(D_MODEL, EXPERT_HIDDEN), weight(D_MODEL, EXPERT_HIDDEN), weight(EXPERT_HIDDEN, D_MODEL)],
            out_specs=pl.BlockSpec(tile, lambda i, blk, na: (i, 0)),
            scratch_shapes=[pltpu.VMEM((D_MODEL, EXPERT_HIDDEN), BF16), pltpu.VMEM((D_MODEL, EXPERT_HIDDEN), BF16),
                            pltpu.VMEM((EXPERT_HIDDEN, D_MODEL), BF16)]),
        out_shape=jax.ShapeDtypeStruct(xin.shape, F32),
        compiler_params=_cparams(),
        name="moe_experts",
    )(blk_e, nact, xin, w1_all, w3_all, w2_all)


COMBINE_TILE = 256


def _combine_kernel(dest_ref, y_ref, meta_ref, h_ref, g2_ref, lng_ref, lnb_ref, o_ref, buf, sem, *, alpha):
    t = COMBINE_TILE
    i = pl.program_id(0)
    last = pl.num_programs(0) - 1

    def copy(step, half, k, slot):
        src = _row_tile(y_ref, dest_ref[slot * (dest_ref.shape[0] // TOP_K) + step * t + k])
        return pltpu.make_async_copy(src, _row_tile(buf.at[half, slot], k), sem.at[half])

    def issue(step, half):
        def body(k, carry):
            for slot in range(TOP_K):
                copy(step, half, k, slot).start(priority=slot)
            return carry
        lax.fori_loop(0, t, body, 0, unroll=DMA_LOOP_UNROLL)

    @pl.when(i == 0)
    def _():
        issue(0, 0)

    @pl.when(i < last)
    def _():
        issue(i + 1, (i + 1) % 2)

    half = i % 2

    def drain(k, carry):
        for slot in range(TOP_K):
            copy(i, half, k, slot).wait()
        return carry

    lax.fori_loop(0, t, drain, 0, unroll=DMA_LOOP_UNROLL)
    meta = meta_ref[...]
    f = meta[:, 2:3] * _rows_from_tiles(buf.at[half, 0]) + meta[:, 3:4] * _rows_from_tiles(buf.at[half, 1])
    o_ref[...] = _layer_norm(alpha * h_ref[...] + g2_ref[...] * f) * lng_ref[...] + lnb_ref[...]


def _combine(dest_flat, yout, meta, h1, mods, ln_g, ln_b, tps, alpha):
    n = h1.shape[0]
    t = COMBINE_TILE
    d = D_MODEL

    def mod_map(i, dest):
        return (jnp.where(i % tps == 0, 8, i // tps), 0, 5)

    return pl.pallas_call(
        functools.partial(_combine_kernel, alpha=alpha),
        grid_spec=pltpu.PrefetchScalarGridSpec(
            num_scalar_prefetch=1,
            grid=(n // t,),
            in_specs=[pl.BlockSpec(memory_space=pl.ANY),
                      pl.BlockSpec((t, LANES), lambda i, dest: (i, 0)),
                      pl.BlockSpec((t, d), lambda i, dest: (i, 0)),
                      pl.BlockSpec((None, 1, d), mod_map),
                      pl.BlockSpec((1, d), lambda i, dest: (0, 0)),
                      pl.BlockSpec((1, d), lambda i, dest: (0, 0))],
            out_specs=pl.BlockSpec((t, d), lambda i, dest: (i, 0)),
            scratch_shapes=[pltpu.VMEM((2, TOP_K, t * ROW_CHUNKS, LANES), F32), pltpu.SemaphoreType.DMA((2,))]),
        out_shape=jax.ShapeDtypeStruct((n, d), F32),
        compiler_params=_cparams(dimension_semantics=("arbitrary",)),
        name="moe_combine",
    )(dest_flat, yout, meta, h1, mods, ln_g, ln_b)


def _prep_w_in(w_in):
    cq, ckv, kr, z, xbc, dt, lx, lg, gate = _split_sections(w_in)
    zeros = lambda w: jnp.zeros(w_in.shape[:-1] + (w,), w_in.dtype)
    out = jnp.concatenate([z, lx, lg, gate, xbc, cq, ckv, kr, zeros(64),
                           dt[..., :SSD_HEADS], zeros(DT_DIR_STRIDE - SSD_HEADS),
                           dt[..., SSD_HEADS:], zeros(DT_DIR_STRIDE - SSD_HEADS)], axis=-1)
    return out.astype(BF16)


def _split_sections(w):
    sizes = (Q_LORA, KV_LORA, QK_ROPE, SSD_INNER, SSD_INNER + 2 * SSD_GROUPS * SSD_STATE, 2 * SSD_HEADS,
             LRU_WIDTH, LRU_WIDTH, N_BRANCH * D_MODEL)
    out, start = [], 0
    for size in sizes:
        out.append(w[..., start:start + size])
        start += size
    return out


def _dir_row(v):
    row = jnp.zeros((LANES,), F32)
    row = row.at[0:SSD_HEADS].set(v[0]).at[DT_DIR_STRIDE:DT_DIR_STRIDE + SSD_HEADS].set(v[1])
    return row[None, :]


def _expand_matrix(rev):
    rows = jnp.arange(LANES)[:, None]
    cols = jnp.arange(SSD_INNER)[None, :]
    base = DT_DIR_STRIDE if rev else 0
    return (rows - base == cols // SSD_HEADDIM).astype(BF16)


def _block_diag_gates(wa, wx):
    per = LRU_GROUP // LRU_BW
    eye = jnp.eye(per, dtype=wa.dtype)

    def bd(w):
        w = w.reshape(LRU_WIDTH // LRU_GROUP, per, LRU_BW, LRU_BW)
        return jnp.einsum('gicd,ij->gicjd', w, eye).reshape(LRU_WIDTH // LRU_GROUP, LRU_GROUP, LRU_GROUP)

    return jnp.concatenate([bd(wa), bd(wx)], axis=-1).astype(BF16)


def _rope_tables(n_ctx, seq):
    rows = seq // GRID_W
    row_pos = jnp.repeat(jnp.arange(rows, dtype=F32), GRID_W)
    col_pos = (jnp.arange(rows * GRID_W) % GRID_W).astype(F32)
    inv_freq = ROPE_THETA ** (-jnp.arange(ROPE_FREQS, dtype=F32) / ROPE_FREQS)
    ang = [row_pos[:, None] * inv_freq, col_pos[:, None] * inv_freq]
    cos = jnp.concatenate([jnp.cos(ang[0]), jnp.cos(ang[0]), jnp.cos(ang[1]), jnp.cos(ang[1])], axis=1)
    sin = jnp.concatenate([-jnp.sin(ang[0]), jnp.sin(ang[0]), -jnp.sin(ang[1]), jnp.sin(ang[1])], axis=1)
    pad = lambda t, fill: jnp.concatenate(
        [jnp.concatenate([jnp.full((n_ctx, QK_ROPE), fill, F32), t], axis=0),
         jnp.zeros((n_ctx + seq, LANES - QK_ROPE), F32)], axis=1)
    return pad(cos, 1.0), pad(sin, 0.0)


def kernel(x, c, ctx, c_ctx, w_mod, b_mod, w_in, q_norm_w, kv_norm_w, w_uq, w_ukv, ssd_conv_w, ssd_conv_b, ssd_a_log, ssd_dt_bias, ssd_d, ssd_norm_w, lru_conv_w, lru_conv_b, lru_wa, lru_ba, lru_wx, lru_bx, lru_lambda, w_branch, w_out, ln1_g, ln1_b, ln2_g, ln2_b, router_wg, router_bg, router_we, router_be, exp_w1, exp_w3, exp_w2):
    bsz, seq, d = x.shape
    n_ctx = ctx.shape[1]
    depth = w_mod.shape[0]
    assert d == D_MODEL and n_ctx == ROW_TILE and seq % ROW_TILE == 0 and bsz == SUBLANES
    s_all = n_ctx + seq
    tps = s_all // ROW_TILE
    n = bsz * s_all
    alpha = (2 * depth) ** 0.25

    h = jnp.concatenate([ctx, x], axis=1).reshape(n, d)
    cvec = jnp.zeros((16, d), F32).at[:bsz].set(c).at[bsz].set(c_ctx)
    mods_all = _mod_vectors(cvec, w_mod, b_mod).reshape(depth, 16, 1, 6 * d)
    cos_t, sin_t = _rope_tables(n_ctx, seq)
    expand = (_expand_matrix(False), _expand_matrix(True))

    w_in_p = _prep_w_in(w_in)
    wq = jnp.pad(w_uq.reshape(depth, Q_LORA, MLA_HEADS, QK_NOPE + QK_ROPE),
                 ((0, 0), (0, 0), (0, 0), (0, QK_PAD - QK_NOPE - QK_ROPE))).reshape(depth, Q_LORA, -1).astype(BF16)
    wkv4 = w_ukv.reshape(depth, KV_LORA, MLA_HEADS, QK_NOPE + V_DIM)
    wkv = jnp.concatenate([wkv4[..., :QK_NOPE].reshape(depth, KV_LORA, -1),
                           wkv4[..., QK_NOPE:].reshape(depth, KV_LORA, -1)], axis=-1).astype(BF16)
    w_router = jnp.concatenate([router_wg, router_we,
                                jnp.zeros((depth, d, LANES - N_GROUPS - N_EXPERTS), F32)], axis=-1)
    b_router = jnp.concatenate([router_bg, router_be,
                                jnp.zeros((depth, LANES - N_GROUPS - N_EXPERTS), F32)], axis=-1)
    wb_all = w_branch.astype(BF16)
    wo_all = w_out.astype(BF16)

    n_blocks = -(-(n * TOP_K) // EXPERT_BLOCK) + N_EXPERTS
    row1 = lambda v: v.reshape(1, -1)

    for l in range(depth):
        mods = mods_all[l]
        proj, misc = _in_proj(h, mods, w_in_p, l, tps)
        proj3 = proj.reshape(bsz, s_all, -1)
        misc3 = misc.reshape(bsz, s_all, -1)

        q, k, vt = _mla_prep(proj, misc, row1(q_norm_w[l]), row1(kv_norm_w[l]), wq, wkv, l, cos_t, sin_t, tps)
        att = _attention(q.reshape(bsz, s_all, -1), k.reshape(bsz, s_all, -1), vt, n_ctx)

        xbc_conv = _dwconv(proj3, COL_XBC, 2 * SSD_INNER, ssd_conv_w[l], row1(ssd_conv_b[l]), True)
        dtb = _dir_row(ssd_dt_bias[l])
        aneg = _dir_row(-jnp.exp(ssd_a_log[l]))
        ys = _ssd_scan(xbc_conv, misc3, dtb, aneg, expand, n_ctx)

        lx_conv = _dwconv(proj3, COL_LX, LRU_WIDTH, lru_conv_w[l], row1(lru_conv_b[l]), False)
        hs = [_lru_scan(lx_conv, _block_diag_gates(lru_wa[l, dr], lru_wx[l, dr]), row1(lru_ba[l, dr]),
                        row1(lru_bx[l, dr]), row1(lru_lambda[l, dr]), n_ctx, bool(dr)) for dr in range(2)]

        d_skip_row = row1(jnp.repeat(ssd_d[l], SSD_HEADDIM))
        h1, u2t = _merge(att.reshape(n, -1), ys[0].reshape(n, -1), ys[1].reshape(n, -1),
                         xbc_conv.reshape(n, -1), proj, hs[0].reshape(n, -1), hs[1].reshape(n, -1), h,
                         d_skip_row, row1(ssd_norm_w[l]), wb_all, wo_all, l, mods,
                         row1(ln1_g[l]), row1(ln1_b[l]), tps, alpha)

        meta, meta_t, cnt = _router(u2t, w_router[l], row1(b_router[l]))
        counts = cnt[0, :N_EXPERTS].astype(jnp.int32)
        padded = (counts + EXPERT_BLOCK - 1) // EXPERT_BLOCK * EXPERT_BLOCK
        pend = jnp.cumsum(padded)
        pstart = pend - padded
        eid = meta_t[0:TOP_K].astype(jnp.int32)
        rank = meta_t[4:4 + TOP_K].astype(jnp.int32)
        seg_start = jnp.zeros_like(eid)
        for e in range(N_EXPERTS):
            seg_start = seg_start + jnp.where(eid == e, pstart[e], 0)
        dest = (seg_start + rank).reshape(-1)
        blk_start = jnp.arange(n_blocks, dtype=jnp.int32) * EXPERT_BLOCK
        blk_e = jnp.minimum(jnp.sum(pend[None, :] <= blk_start[:, None], axis=1), N_EXPERTS - 1).astype(jnp.int32)
        nact = (pend[-1:] // EXPERT_BLOCK).astype(jnp.int32)
        blk_e = jnp.where(jnp.arange(n_blocks) < nact[0], blk_e, jnp.take(blk_e, jnp.maximum(nact[0] - 1, 0)))

        xin = _dispatch(dest, pend.astype(jnp.int32), u2t, n_blocks * EXPERT_BLOCK)
        yout = _experts(blk_e, nact, xin, exp_w1, exp_w3, exp_w2, l)
        h = _combine(dest, yout, meta, h1, mods, row1(ln2_g[l]), row1(ln2_b[l]), tps, alpha)

    return h.reshape(bsz, s_all, d)[:, n_ctx:, :]
```

```python
import functools
import math

import jax
import jax.numpy as jnp
from jax import lax
from jax.experimental import pallas as pl
from jax.experimental.pallas import tpu as pltpu

F32 = jnp.float32
BF16 = jnp.bfloat16

D_MODEL = 1024
GRID_W = 64
N_BRANCH = 3

MLA_HEADS = 8
QK_NOPE = 128
QK_ROPE = 64
V_DIM = 128
Q_LORA = 512
KV_LORA = 256
ROPE_FREQS = QK_ROPE // 4
ROPE_THETA = 10000.0
ATTN_SCALE = (QK_NOPE + QK_ROPE) ** -0.5
QK_PAD = 256

SSD_HEADDIM = 64
SSD_INNER = 1024
SSD_HEADS = 16
SSD_GROUPS = 4
SSD_STATE = 128
SSD_CHUNK = 128
GROUP_W = SSD_INNER // SSD_GROUPS

CONV_W = 4
LRU_WIDTH = 1024
LRU_BW = 64
LRU_C = 8.0
LRU_TILE = 128
LRU_GROUP = 256

N_GROUPS = 4
EXPERTS_PER_GROUP = 8
N_EXPERTS = 32
TOP_K = 2
EXPERT_HIDDEN = 512
EXPERT_BLOCK = 256

LN_EPS = 1e-5
RMS_EPS = 1e-6

LANES = 128
SUBLANES = 8
ROW_TILE = 256
ROW_CHUNKS = D_MODEL // LANES
VMEM_LIMIT = 56 * 1024 * 1024

COL_Z = 0
COL_LX = 1024
COL_LG = 2048
COL_GATE = 3072
COL_XBC = 6144
COL_CQ = 8192
COL_CKV = 8704
COL_KR = 8960
COL_DT = 9088
DT_DIR_STRIDE = 64
IN_COLS_PAD = 9216


def _cparams(**kw):
    return pltpu.CompilerParams(vmem_limit_bytes=VMEM_LIMIT, **kw)


def _split3(x):
    hi = x.astype(BF16)
    r = x - hi.astype(F32)
    mid = r.astype(BF16)
    lo = (r - mid.astype(F32)).astype(BF16)
    return hi, mid, lo


def _dot(a, b):
    return jnp.dot(a, b, preferred_element_type=F32)


def _dot_nt(a, b):
    return lax.dot_general(a, b, (((1,), (1,)), ((), ())), preferred_element_type=F32)


def _sigmoid(x):
    return 1.0 / (1.0 + jnp.exp(-x))


def _sigmoid_tanh(x):
    return 0.5 * jnp.tanh(0.5 * x) + 0.5


def _silu(x):
    return x * _sigmoid(x)


def _softplus(x):
    return jnp.maximum(x, 0.0) + jnp.log(1.0 + jnp.exp(-jnp.abs(x)))


def _layer_norm(t):
    mu = jnp.mean(t, axis=-1, keepdims=True)
    c = t - mu
    var = jnp.mean(c * c, axis=-1, keepdims=True)
    return c * lax.rsqrt(var + LN_EPS)


def _mod_kernel(c_ref, w_ref, b_ref, o_ref):
    s = _silu(c_ref[...]).astype(BF16)
    o_ref[...] = _dot(s, w_ref[...].astype(BF16)) + b_ref[...]


def _mod_vectors(cvec, w_mod, b_mod):
    depth, d, n = w_mod.shape
    tn = 1536
    return pl.pallas_call(
        _mod_kernel,
        grid=(depth, n // tn),
        in_specs=[pl.BlockSpec((16, d), lambda l, j: (0, 0)),
                  pl.BlockSpec((None, d, tn), lambda l, j: (l, 0, j)),
                  pl.BlockSpec((None, 1, tn), lambda l, j: (l, 0, j))],
        out_specs=pl.BlockSpec((None, 16, tn), lambda l, j: (l, 0, j)),
        out_shape=jax.ShapeDtypeStruct((depth, 16, n), F32),
        compiler_params=_cparams(),
        name="mod_vectors",
    )(cvec, w_mod, b_mod.reshape(depth, 1, n))


def _mod_spec(tiles_per_sample, k):
    def imap(i):
        return (jnp.where(i % tiles_per_sample == 0, 8, i // tiles_per_sample), 0, k)
    return pl.BlockSpec((None, 1, D_MODEL), imap)


IN_PROJ_COLS = 1024


def _in_proj_kernel(h_ref, sh_ref, sc_ref, w_ref, o_ref, misc_ref):
    u = (_layer_norm(h_ref[...]) * (1.0 + sc_ref[...]) + sh_ref[...]).astype(BF16)
    main = o_ref.shape[1]
    for c0 in range(0, main, IN_PROJ_COLS):
        cs = slice(c0, min(c0 + IN_PROJ_COLS, main))
        o_ref[:, cs] = _dot(u, w_ref[:, cs]).astype(o_ref.dtype)
    misc_ref[...] = _dot(u, w_ref[:, main:])


def _in_proj(h, mods, w_all, layer, tps):
    n = h.shape[0]
    _, k, cols = w_all.shape
    return pl.pallas_call(
        _in_proj_kernel,
        grid=(n // ROW_TILE,),
        in_specs=[pl.BlockSpec((ROW_TILE, k), lambda i: (i, 0)),
                  _mod_spec(tps, 0), _mod_spec(tps, 1),
                  pl.BlockSpec((None, k, cols), lambda i: (layer, 0, 0), pipeline_mode=pl.Buffered(1))],
        out_specs=[pl.BlockSpec((ROW_TILE, COL_KR), lambda i: (i, 0)),
                   pl.BlockSpec((ROW_TILE, cols - COL_KR), lambda i: (i, 0))],
        out_shape=[jax.ShapeDtypeStruct((n, COL_KR), BF16),
                   jax.ShapeDtypeStruct((n, cols - COL_KR), F32)],
        compiler_params=_cparams(),
        name="in_proj",
    )(h, mods, mods, w_all)


def _rope128(t, cos, sin):
    lane = lax.broadcasted_iota(jnp.int32, t.shape, 1)
    partner = jnp.where(lane % (2 * ROPE_FREQS) < ROPE_FREQS,
                        pltpu.roll(t, LANES - ROPE_FREQS, 1), pltpu.roll(t, ROPE_FREQS, 1))
    return t * cos + partner * sin


def _rms(t, gain):
    return t * lax.rsqrt(jnp.mean(t * t, axis=-1, keepdims=True) + RMS_EPS) * gain


def _mla_prep_kernel(cq_ref, ckv_ref, kr_ref, qg_ref, kvg_ref, wq_ref, wkv_ref, cos_ref, sin_ref,
                     q_ref, k_ref, vt_ref):
    cos = cos_ref[...]
    sin = sin_ref[...]
    q = _dot(_rms(cq_ref[...].astype(F32), qg_ref[...]).astype(BF16), wq_ref[...]) * ATTN_SCALE
    kv = _dot(_rms(ckv_ref[...].astype(F32), kvg_ref[...]).astype(BF16), wkv_ref[...])
    krz = _rope128(kr_ref[...], cos, sin).astype(BF16)
    for h in range(MLA_HEADS):
        c0 = h * QK_PAD
        q_ref[:, c0:c0 + QK_NOPE] = q[:, c0:c0 + QK_NOPE].astype(BF16)
        q_ref[:, c0 + QK_NOPE:c0 + QK_PAD] = _rope128(q[:, c0 + QK_NOPE:c0 + QK_PAD], cos, sin).astype(BF16)
        k_ref[:, c0:c0 + QK_NOPE] = kv[:, h * QK_NOPE:(h + 1) * QK_NOPE].astype(BF16)
        k_ref[:, c0 + QK_NOPE:c0 + QK_PAD] = krz
    vt_ref[...] = kv[:, MLA_HEADS * QK_NOPE:].T.astype(BF16)


def _mla_prep(proj, misc, q_gain, kv_gain, wq_all, wkv_all, layer, cos_t, sin_t, tps):
    n = proj.shape[0]
    t = ROW_TILE
    row = lambda w, cb: pl.BlockSpec((t, w), lambda i: (i, cb))
    const = lambda shape: pl.BlockSpec(shape, lambda i: (0, 0))
    stacked = lambda shape: pl.BlockSpec((None,) + shape, lambda i: (layer, 0, 0))
    return pl.pallas_call(
        _mla_prep_kernel,
        grid=(n // t,),
        in_specs=[row(Q_LORA, COL_CQ // Q_LORA), row(KV_LORA, COL_CKV // KV_LORA), row(LANES, 0),
                  const((1, Q_LORA)), const((1, KV_LORA)),
                  stacked((Q_LORA, MLA_HEADS * QK_PAD)), stacked((KV_LORA, 2 * MLA_HEADS * QK_NOPE)),
                  pl.BlockSpec((t, LANES), lambda i: (i % tps, 0)),
                  pl.BlockSpec((t, LANES), lambda i: (i % tps, 0))],
        out_specs=[pl.BlockSpec((t, MLA_HEADS * QK_PAD), lambda i: (i, 0)),
                   pl.BlockSpec((t, MLA_HEADS * QK_PAD), lambda i: (i, 0)),
                   pl.BlockSpec((None, MLA_HEADS * V_DIM, t), lambda i: (i // tps, 0, i % tps))],
        out_shape=[jax.ShapeDtypeStruct((n, MLA_HEADS * QK_PAD), BF16),
                   jax.ShapeDtypeStruct((n, MLA_HEADS * QK_PAD), BF16),
                   jax.ShapeDtypeStruct((n // (tps * t), MLA_HEADS * V_DIM, tps * t), BF16)],
        compiler_params=_cparams(),
        name="mla_prep",
    )(proj, proj, misc, q_gain, kv_gain, wq_all, wkv_all, cos_t, sin_t)


ATTN_KEY_CHUNK = 1152


def _attn_kernel(q_ref, k_ref, vt_ref, o_ref, s0_ref, s1_ref, *, n_ctx):
    s_all = k_ref.shape[0]
    tq = ROW_TILE
    n_tiles = (s_all - n_ctx) // tq
    chunks = [(c, min(c + ATTN_KEY_CHUNK, s_all)) for c in range(0, s_all, ATTN_KEY_CHUNK)]

    def scores(s_ref, r0, key_chunks):
        q = q_ref[pl.ds(r0, tq), :]
        for c0, c1 in key_chunks:
            s_ref[c0:c1, :] = _dot_nt(k_ref[c0:c1, :], q)

    def finish(s_ref, r0, key_chunks):
        m = None
        for c0, c1 in key_chunks:
            cm = jnp.max(s_ref[c0:c1, :], axis=0, keepdims=True)
            m = cm if m is None else jnp.maximum(m, cm)
        l = acc = None
        for c0, c1 in key_chunks:
            p = jnp.exp(s_ref[c0:c1, :] - m)
            ps = jnp.sum(p, axis=0, keepdims=True)
            pv = _dot(vt_ref[:, c0:c1], p.astype(BF16))
            l, acc = (ps, pv) if l is None else (l + ps, acc + pv)
        o_ref[pl.ds(r0, tq), :] = (acc / l).T.astype(o_ref.dtype)

    scores(s0_ref, 0, [(0, n_ctx)])
    finish(s0_ref, 0, [(0, n_ctx)])
    scores(s0_ref, n_ctx, chunks)

    def body(j, carry):
        r_a = pl.multiple_of(n_ctx + 2 * j * tq, tq)
        r_b = pl.multiple_of(r_a + tq, tq)
        r_c = pl.multiple_of(jnp.minimum(r_b + tq, s_all - tq), tq)
        scores(s1_ref, r_b, chunks)
        finish(s0_ref, r_a, chunks)
        scores(s0_ref, r_c, chunks)
        finish(s1_ref, r_b, chunks)
        return carry

    assert n_tiles % 2 == 0
    lax.fori_loop(0, n_tiles // 2, body, 0)


def _attention(q, k, vt, n_ctx):
    b, s, _ = q.shape
    return pl.pallas_call(
        functools.partial(_attn_kernel, n_ctx=n_ctx),
        grid=(b, MLA_HEADS),
        in_specs=[pl.BlockSpec((None, s, QK_PAD), lambda bi, h: (bi, 0, h)),
                  pl.BlockSpec((None, s, QK_PAD), lambda bi, h: (bi, 0, h)),
                  pl.BlockSpec((None, V_DIM, s), lambda bi, h: (bi, h, 0))],
        out_specs=pl.BlockSpec((None, s, V_DIM), lambda bi, h: (bi, 0, h)),
        out_shape=jax.ShapeDtypeStruct((b, s, MLA_HEADS * V_DIM), BF16),
        scratch_shapes=[pltpu.VMEM((s, ROW_TILE), F32), pltpu.VMEM((s, ROW_TILE), F32)],
        compiler_params=_cparams(),
        name="attention",
    )(q, k, vt)


def _conv_kernel(x_ref, p_ref, n_ref, w_ref, b_ref, o_ref, *, silu, tps):
    j = pl.program_id(1)
    has_prev = (j > 1).astype(F32)
    has_next = jnp.logical_and(j > 0, j < tps - 1).astype(F32)
    x = x_ref[...].astype(F32)
    xe = jnp.concatenate([p_ref[...].astype(F32) * has_prev, x, n_ref[...].astype(F32) * has_next], axis=0)
    rows = xe.shape[0]
    t = x_ref.shape[0]
    w = w_ref[...]
    y = b_ref[...] + w[2:3, :] * x
    y = y + w[0:1, :] * pltpu.roll(xe, 2, 0)[CONV_HALO:CONV_HALO + t]
    y = y + w[1:2, :] * pltpu.roll(xe, 1, 0)[CONV_HALO:CONV_HALO + t]
    y = y + w[3:4, :] * pltpu.roll(xe, rows - 1, 0)[CONV_HALO:CONV_HALO + t]
    o_ref[...] = (_silu(y) if silu else y).astype(o_ref.dtype)


CONV_HALO = 16


def _dwconv(proj3, col0, width, w, bias, silu):
    b, s, _ = proj3.shape
    t = ROW_TILE
    tc = 1024
    cb = col0 // tc
    tps = s // t
    hb = t // CONV_HALO
    last = s // CONV_HALO - 1
    return pl.pallas_call(
        functools.partial(_conv_kernel, silu=silu, tps=tps),
        grid=(b, tps, width // tc),
        in_specs=[pl.BlockSpec((None, t, tc), lambda bi, j, c: (bi, j, cb + c)),
                  pl.BlockSpec((None, CONV_HALO, tc), lambda bi, j, c: (bi, jnp.maximum(j * hb - 1, 0), cb + c)),
                  pl.BlockSpec((None, CONV_HALO, tc), lambda bi, j, c: (bi, jnp.minimum((j + 1) * hb, last), cb + c)),
                  pl.BlockSpec((CONV_W, tc), lambda bi, j, c: (0, c)),
                  pl.BlockSpec((1, tc), lambda bi, j, c: (0, c))],
        out_specs=pl.BlockSpec((None, t, tc), lambda bi, j, c: (bi, j, c)),
        out_shape=jax.ShapeDtypeStruct((b, s, width), BF16),
        compiler_params=_cparams(),
        name="dwconv",
    )(proj3, proj3, proj3, w, bias)


def _ssd_kernel(xs0, b0, c0, dt0, xs1, b1, c1, dt1, dtb_ref, aneg_ref, e0_ref, e1_ref, y0_ref, y1_ref,
                h0_ref, h1_ref):
    @pl.when(pl.program_id(1) == 0)
    def _():
        h0_ref[...] = jnp.zeros_like(h0_ref)
        h1_ref[...] = jnp.zeros_like(h1_ref)

    _ssd_chunk(xs0, b0, c0, dt0, dtb_ref, aneg_ref, e0_ref, y0_ref, h0_ref, rev=False)
    _ssd_chunk(xs1, b1, c1, dt1, dtb_ref, aneg_ref, e1_ref, y1_ref, h1_ref, rev=True)


def _ssd_chunk(xs_ref, b_ref, c_ref, dt_ref, dtb_ref, aneg_ref, e_ref, y_ref, h_ref, *, rev):
    q = SSD_CHUNK
    dt = _softplus(dt_ref[...] + dtb_ref[...])
    a = dt * aneg_ref[...]
    row = lax.broadcasted_iota(jnp.int32, (q, q), 0)
    col = lax.broadcasted_iota(jnp.int32, (q, q), 1)
    tri = (col >= row) if rev else (col <= row)
    tri_b = jnp.where(tri, 1.0, 0.0).astype(BF16)
    a3 = _split3(a)
    acum = _dot(tri_b, a3[0]) + _dot(tri_b, a3[1]) + _dot(tri_b, a3[2])
    total = jnp.sum(a, axis=0, keepdims=True)
    stack = jnp.concatenate([dt, acum, jnp.broadcast_to(total, (SUBLANES, LANES))], axis=0)
    e = e_ref[...]
    s3 = _split3(stack)
    ex = _dot(s3[0], e) + _dot(s3[1], e) + _dot(s3[2], e)
    dt_e = ex[0:q]
    acum_e = ex[q:2 * q]
    total_e = ex[2 * q:2 * q + 1]
    to_end_e = jnp.exp(total_e - acum_e)
    from_start_e = jnp.exp(acum_e)
    chunk_decay_e = jnp.exp(total_e)
    xg = xs_ref[...].astype(F32) * dt_e
    xg_b = xg.astype(BF16)
    w_b = (xg * to_end_e).astype(BF16)
    acum_t = acum.T
    base = DT_DIR_STRIDE if rev else 0
    for g in range(SSD_GROUPS):
        gs = slice(g * GROUP_W, (g + 1) * GROUP_W)
        bg = b_ref[:, g * SSD_STATE:(g + 1) * SSD_STATE]
        cg = c_ref[:, g * SSD_STATE:(g + 1) * SSD_STATE]
        cb = _dot_nt(cg, bg)
        h_prev = h_ref[:, gs]
        y_off = _dot(cg, h_prev.astype(BF16)) * from_start_e[:, gs]
        h_ref[:, gs] = chunk_decay_e[:, gs] * h_prev + _dot(bg.astype(F32).T.astype(BF16), w_b[:, gs])
        parts = []
        for hh in range(SSD_HEADS // SSD_GROUPS):
            head = g * (SSD_HEADS // SSD_GROUPS) + hh
            c = base + head
            seg = acum[:, c:c + 1] - acum_t[c:c + 1, :]
            lmat = jnp.exp(jnp.where(tri, seg, -1e30))
            parts.append(_dot((cb * lmat).astype(BF16), xg_b[:, head * SSD_HEADDIM:(head + 1) * SSD_HEADDIM]))
        y_ref[:, gs] = (jnp.concatenate(parts, axis=1) + y_off).astype(y_ref.dtype)


def _ssd_order(step, n_ctx_chunks, n_chunks, rev):
    if not rev:
        return step
    return jnp.where(step < n_ctx_chunks, n_ctx_chunks - 1 - step, n_chunks - 1 + n_ctx_chunks - step)


def _ssd_scan(xbc_conv, misc3, dt_bias_row, a_neg_row, expand, n_ctx):
    b, s, _ = xbc_conv.shape
    q = SSD_CHUNK
    nchunks = s // q
    const = lambda shape: pl.BlockSpec(shape, lambda bi, i: (0, 0))
    specs, out_specs = [], []
    for rev in (False, True):
        order = functools.partial(_ssd_order, n_ctx_chunks=n_ctx // q, n_chunks=nchunks, rev=rev)
        chunk = lambda w, cb, order=order: pl.BlockSpec((None, q, w), lambda bi, i: (bi, order(i), cb))
        specs += [chunk(SSD_INNER, 0), chunk(SSD_GROUPS * SSD_STATE, 2), chunk(SSD_GROUPS * SSD_STATE, 3),
                  chunk(LANES, (COL_DT - COL_KR) // LANES)]
        out_specs.append(chunk(SSD_INNER, 0))
    return pl.pallas_call(
        _ssd_kernel,
        grid=(b, nchunks),
        in_specs=specs + [const((1, LANES)), const((1, LANES)), const((LANES, SSD_INNER)), const((LANES, SSD_INNER))],
        out_specs=out_specs,
        out_shape=[jax.ShapeDtypeStruct((b, s, SSD_INNER), BF16)] * 2,
        scratch_shapes=[pltpu.VMEM((SSD_STATE, SSD_INNER), F32)] * 2,
        compiler_params=_cparams(),
        name="ssd_scan",
    )(xbc_conv, xbc_conv, xbc_conv, misc3, xbc_conv, xbc_conv, xbc_conv, misc3,
      dt_bias_row, a_neg_row, expand[0], expand[1])


def _lru_kernel(x_ref, w_ref, ba_ref, bx_ref, lam_ref, o_ref, a_s, b_s, h_s, carry, *, rev):
    nb, t, _ = x_ref.shape
    per = LRU_GROUP // LANES

    @pl.when(pl.program_id(0) == 0)
    def _():
        carry[...] = jnp.zeros_like(carry)

    decay = -LRU_C * _softplus(-lam_ref[...])
    for bi in range(nb):
        for g in range(LRU_WIDTH // LRU_GROUP):
            gs = slice(g * LRU_GROUP, (g + 1) * LRU_GROUP)
            xd = x_ref[bi, :, gs]
            ri = _dot(xd, w_ref[g])
            r = _sigmoid_tanh(ri[:, :LRU_GROUP] + ba_ref[:, gs])
            i = _sigmoid_tanh(ri[:, LRU_GROUP:] + bx_ref[:, gs])
            log_a = decay[:, gs] * r
            a = jnp.exp(log_a)
            bt = jnp.sqrt(jnp.tanh(-log_a) * (1.0 + a * a)) * (i * xd.astype(F32))
            for cc in range(per):
                ls = slice(cc * LANES, (cc + 1) * LANES)
                a_s[g * per + cc, pl.ds(bi, t, stride=nb), :] = a[:, ls]
                b_s[g * per + cc, pl.ds(bi, t, stride=nb), :] = bt[:, ls]

    def step(k, h):
        tt = (t - 1 - k) if rev else k
        r0 = pl.multiple_of(tt * nb, nb)
        h = a_s[:, pl.ds(r0, nb), :] * h + b_s[:, pl.ds(r0, nb), :]
        h_s[:, pl.ds(r0, nb), :] = h
        return h

    carry[...] = lax.fori_loop(0, t, step, carry[...])
    for bi in range(nb):
        for cc in range(LRU_WIDTH // LANES):
            o_ref[bi, :, cc * LANES:(cc + 1) * LANES] = h_s[cc, pl.ds(bi, t, stride=nb), :].astype(o_ref.dtype)


def _lru_scan(lx_conv, w_gate, ba, bx, lam, n_ctx, rev):
    b, s, width = lx_conv.shape
    t = LRU_TILE
    ntiles = s // t
    order = functools.partial(_ssd_order, n_ctx_chunks=n_ctx // t, n_chunks=ntiles, rev=rev)
    const = lambda shape: pl.BlockSpec(shape, lambda i: (0,) * len(shape))
    return pl.pallas_call(
        functools.partial(_lru_kernel, rev=rev),
        grid=(ntiles,),
        in_specs=[pl.BlockSpec((b, t, width), lambda i: (0, order(i), 0)),
                  const(w_gate.shape), const((1, width)), const((1, width)), const((1, width))],
        out_specs=pl.BlockSpec((b, t, width), lambda i: (0, order(i), 0)),
        out_shape=jax.ShapeDtypeStruct((b, s, width), BF16),
        scratch_shapes=[pltpu.VMEM((width // LANES, t * b, LANES), F32)] * 3
        + [pltpu.VMEM((width // LANES, b, LANES), F32)],
        compiler_params=_cparams(),
        name="lru_rev" if rev else "lru_fwd",
    )(lx_conv, w_gate, ba, bx, lam)


def _gelu_tanh(x):
    return 0.5 * x * (1.0 + jnp.tanh(math.sqrt(2.0 / math.pi) * (x + 0.044715 * (x * x * x))))


def _merge_kernel(att_ref, y0_ref, y1_ref, xs_ref, z_ref, l0_ref, l1_ref, lg_ref, gate_ref, h_ref,
                  dskip_ref, ssdg_ref, wb_ref, wo_ref, g1_ref, lng_ref, lnb_ref, sh2_ref, sc2_ref,
                  h1_ref, u2_ref, *, alpha):
    f32 = lambda ref: ref[...].astype(F32)
    y = f32(y0_ref) + f32(y1_ref) + f32(xs_ref) * dskip_ref[...]
    z = f32(z_ref)
    y = y * (z * _sigmoid_tanh(z))
    parts = []
    for g in range(SSD_GROUPS):
        yg = y[:, g * GROUP_W:(g + 1) * GROUP_W]
        parts.append(yg * lax.rsqrt(jnp.mean(yg * yg, axis=-1, keepdims=True) + RMS_EPS))
    y_ssd = jnp.concatenate(parts, axis=1) * ssdg_ref[...]
    y_lru = (f32(l0_ref) + f32(l1_ref)) * _gelu_tanh(f32(lg_ref))
    branches = (att_ref[...], y_ssd.astype(BF16), y_lru.astype(BF16))
    mix = None
    for k in range(N_BRANCH):
        gate = gate_ref[:, k * D_MODEL:(k + 1) * D_MODEL].astype(F32)
        term = _sigmoid_tanh(gate) * _dot(branches[k], wb_ref[k])
        mix = term if mix is None else mix + term
    out = _dot(mix.astype(BF16), wo_ref[...])
    h1 = _layer_norm(alpha * h_ref[...] + g1_ref[...] * out) * lng_ref[...] + lnb_ref[...]
    h1_ref[...] = h1
    u2 = _layer_norm(h1) * (1.0 + sc2_ref[...]) + sh2_ref[...]
    for s in range(D_MODEL // LANES):
        u2_ref[pl.ds(s, u2.shape[0], stride=ROW_CHUNKS), :] = u2[:, s * LANES:(s + 1) * LANES]


def _merge(att, y0, y1, xbc_conv, proj, l0, l1, h, d_skip_row, ssd_gain, wb_all, wo_all, layer, mods, ln_g, ln_b,
           tps, alpha):
    n = h.shape[0]
    t = ROW_TILE
    d = D_MODEL
    row = lambda cb: pl.BlockSpec((t, d), lambda i: (i, cb))
    const = lambda shape: pl.BlockSpec(shape, lambda i: (0,) * len(shape))
    stacked = lambda shape: pl.BlockSpec((None,) + shape, lambda i: (layer,) + (0,) * len(shape))
    return pl.pallas_call(
        functools.partial(_merge_kernel, alpha=alpha),
        grid=(n // t,),
        in_specs=[row(0), row(0), row(0), row(0), row(COL_Z // d), row(0), row(0), row(COL_LG // d),
                  pl.BlockSpec((t, N_BRANCH * d), lambda i: (i, COL_GATE // (N_BRANCH * d))), row(0),
                  const((1, d)), const((1, d)), stacked(wb_all.shape[1:]), stacked(wo_all.shape[1:]),
                  _mod_spec(tps, 2), const((1, d)), const((1, d)), _mod_spec(tps, 3), _mod_spec(tps, 4)],
        out_specs=[row(0), pl.BlockSpec((t * ROW_CHUNKS, LANES), lambda i: (i, 0))],
        out_shape=[jax.ShapeDtypeStruct((n, d), F32),
                   jax.ShapeDtypeStruct((n * ROW_CHUNKS, LANES), F32)],
        compiler_params=_cparams(),
        name="merge",
    )(att, y0, y1, xbc_conv, proj, l0, l1, proj, proj, h, d_skip_row, ssd_gain, wb_all, wo_all,
      mods, ln_g, ln_b, mods, mods)


def _row_tile(ref, r):
    return ref.at[pl.ds(pl.multiple_of(r * ROW_CHUNKS, ROW_CHUNKS), ROW_CHUNKS), :]


def _rows_from_tiles(ref):
    rows = ref.shape[0] // ROW_CHUNKS
    return jnp.concatenate([ref[pl.ds(s, rows, stride=ROW_CHUNKS), :] for s in range(ROW_CHUNKS)], axis=1)


def _router_kernel(u_ref, w_ref, b_ref, meta_ref, meta_t_ref, cnt_ref, count):
    @pl.when(pl.program_id(0) == 0)
    def _():
        count[...] = jnp.zeros_like(count)

    u = _rows_from_tiles(u_ref)
    t = u.shape[0]
    uh = u.astype(BF16)
    ul = (u - uh.astype(F32)).astype(BF16)
    w = w_ref[...]
    wh = w.astype(BF16)
    wl = (w - wh.astype(F32)).astype(BF16)
    logits = _dot(uh, wh) + _dot(uh, wl) + _dot(ul, wh) + b_ref[...]
    lane = lax.broadcasted_iota(jnp.int32, logits.shape, 1)
    neg = -jnp.inf
    big = 4 * LANES

    def top1(vals):
        m = jnp.max(vals, axis=-1, keepdims=True)
        idx = jnp.min(jnp.where(vals == m, lane, big), axis=-1, keepdims=True)
        return m, idx

    glog = jnp.where(lane < N_GROUPS, logits, neg)
    gmax, gsel = top1(glog)
    gval = 1.0 / jnp.sum(jnp.exp(glog - gmax), axis=-1, keepdims=True)
    lo = N_GROUPS + gsel * EXPERTS_PER_GROUP
    elog = jnp.where(jnp.logical_and(lane >= lo, lane < lo + EXPERTS_PER_GROUP), logits, neg)
    v1, i1 = top1(elog)
    v2, i2 = top1(jnp.where(lane == i1, neg, elog))
    e21 = jnp.exp(v2 - v1)
    w1 = gval / (1.0 + e21)
    w2 = gval * e21 / (1.0 + e21)
    e1 = i1 - N_GROUPS
    e2 = i2 - N_GROUPS
    onehot = jnp.logical_or(lane == e1, lane == e2)
    oh = jnp.where(onehot, 1.0, 0.0)
    r = lax.broadcasted_iota(jnp.int32, (t, t), 0)
    c = lax.broadcasted_iota(jnp.int32, (t, t), 1)
    before = jnp.where(c < r, 1.0, 0.0).astype(BF16)
    prefix = _dot(before, oh.astype(BF16)) + count[0:1, :]
    rank1 = jnp.sum(jnp.where(lane == e1, prefix, 0.0), axis=-1, keepdims=True)
    rank2 = jnp.sum(jnp.where(lane == e2, prefix, 0.0), axis=-1, keepdims=True)
    meta = jnp.where(lane == 0, e1.astype(F32), 0.0)
    meta = jnp.where(lane == 1, e2.astype(F32), meta)
    meta = jnp.where(lane == 2, w1, meta)
    meta = jnp.where(lane == 3, w2, meta)
    meta = jnp.where(lane == 4, rank1, meta)
    meta = jnp.where(lane == 5, rank2, meta)
    meta_ref[...] = meta
    meta_t_ref[...] = meta.T[0:SUBLANES, :]
    count[...] = count[...] + jnp.sum(oh, axis=0, keepdims=True)
    cnt_ref[...] = count[...]


def _router(u2t, w_router, b_router):
    n = u2t.shape[0] // ROW_CHUNKS
    t = ROW_TILE
    return pl.pallas_call(
        _router_kernel,
        grid=(n // t,),
        in_specs=[pl.BlockSpec((t * ROW_CHUNKS, LANES), lambda i: (i, 0)),
                  pl.BlockSpec((D_MODEL, LANES), lambda i: (0, 0)),
                  pl.BlockSpec((1, LANES), lambda i: (0, 0))],
        out_specs=[pl.BlockSpec((t, LANES), lambda i: (i, 0)),
                   pl.BlockSpec((SUBLANES, t), lambda i: (0, i)),
                   pl.BlockSpec((SUBLANES, LANES), lambda i: (0, 0))],
        out_shape=[jax.ShapeDtypeStruct((n, LANES), F32),
                   jax.ShapeDtypeStruct((SUBLANES, n), F32),
                   jax.ShapeDtypeStruct((SUBLANES, LANES), F32)],
        scratch_shapes=[pltpu.VMEM((SUBLANES, LANES), F32)],
        compiler_params=_cparams(),
        name="router",
    )(u2t, w_router, b_router)


DISPATCH_TILE = 512
DMA_LOOP_UNROLL = 8


def _dispatch_kernel(dest_ref, pend_ref, u_ref, x_ref, zeros, sem, zsem):
    base = pl.program_id(0) * DISPATCH_TILE

    @pl.when(pl.program_id(0) == 0)
    def _():
        zeros[...] = jnp.zeros_like(zeros)

        def tail(e):
            start = pl.multiple_of((pend_ref[e] - EXPERT_BLOCK) * ROW_CHUNKS, EXPERT_BLOCK * ROW_CHUNKS)
            return pltpu.make_async_copy(zeros, x_ref.at[pl.ds(start, EXPERT_BLOCK * ROW_CHUNKS), :], zsem)

        def nonempty(e):
            return pend_ref[e] > jnp.where(e == 0, 0, pend_ref[jnp.maximum(e - 1, 0)])

        def fill(e, carry):
            @pl.when(nonempty(e))
            def _():
                tail(e).start()
            return carry

        def done(e, carry):
            @pl.when(nonempty(e))
            def _():
                tail(e).wait()
            return carry

        lax.fori_loop(0, N_EXPERTS, fill, 0)
        lax.fori_loop(0, N_EXPERTS, done, 0)

        n_blocks = x_ref.shape[0] // (EXPERT_BLOCK * ROW_CHUNKS)
        n_active = pend_ref[N_EXPERTS - 1] // EXPERT_BLOCK

        def unused(b):
            start = pl.multiple_of(b * (EXPERT_BLOCK * ROW_CHUNKS), EXPERT_BLOCK * ROW_CHUNKS)
            return pltpu.make_async_copy(zeros, x_ref.at[pl.ds(start, EXPERT_BLOCK * ROW_CHUNKS), :], zsem)

        lax.fori_loop(n_active, n_blocks, lambda b, c: (unused(b).start(), c)[1], 0)
        lax.fori_loop(n_active, n_blocks, lambda b, c: (unused(b).wait(), c)[1], 0)

    def copy(k, slot):
        n_tok = dest_ref.shape[0] // TOP_K
        return pltpu.make_async_copy(_row_tile(u_ref, k), _row_tile(x_ref, dest_ref[slot * n_tok + base + k]), sem)

    def issue(k, carry):
        for slot in range(TOP_K):
            copy(k, slot).start(priority=slot)
        return carry

    def drain(k, carry):
        for slot in range(TOP_K):
            copy(k, slot).wait()
        return carry

    lax.fori_loop(0, DISPATCH_TILE, issue, 0, unroll=DMA_LOOP_UNROLL)
    lax.fori_loop(0, DISPATCH_TILE, drain, 0, unroll=DMA_LOOP_UNROLL)


def _dispatch(dest_flat, pend, u2t, n_rows):
    n = u2t.shape[0] // ROW_CHUNKS
    return pl.pallas_call(
        _dispatch_kernel,
        grid_spec=pltpu.PrefetchScalarGridSpec(
            num_scalar_prefetch=2,
            grid=(n // DISPATCH_TILE,),
            in_specs=[pl.BlockSpec((DISPATCH_TILE * ROW_CHUNKS, LANES), lambda i, dest, pend: (i, 0))],
            out_specs=pl.BlockSpec(memory_space=pl.ANY),
            scratch_shapes=[pltpu.VMEM((EXPERT_BLOCK * ROW_CHUNKS, LANES), u2t.dtype),
                            pltpu.SemaphoreType.DMA(()), pltpu.SemaphoreType.DMA(())]),
        out_shape=jax.ShapeDtypeStruct((n_rows * ROW_CHUNKS, LANES), u2t.dtype),
        compiler_params=_cparams(dimension_semantics=("arbitrary",)),
        name="moe_dispatch",
    )(dest_flat, pend, u2t)


def _expert_kernel(blk_ref, nact_ref, x_ref, w1_ref, w3_ref, w2_ref, y_ref, w1_b, w3_b, w2_b):
    i = pl.program_id(0)

    @pl.when(jnp.logical_or(i == 0, blk_ref[i] != blk_ref[jnp.maximum(i - 1, 0)]))
    def _():
        w1_b[...] = w1_ref[...].astype(BF16)
        w3_b[...] = w3_ref[...].astype(BF16)
        w2_b[...] = w2_ref[...].astype(BF16)

    @pl.when(i < nact_ref[0])
    def _():
        x = _rows_from_tiles(x_ref).astype(BF16)
        hid = _silu(_dot(x, w1_b[...])) * _dot(x, w3_b[...])
        y = _dot(hid.astype(BF16), w2_b[...])
        for s in range(ROW_CHUNKS):
            y_ref[pl.ds(s, EXPERT_BLOCK, stride=ROW_CHUNKS), :] = y[:, s * LANES:(s + 1) * LANES]

    @pl.when(i >= nact_ref[0])
    def _():
        y_ref[...] = jnp.zeros_like(y_ref)


def _experts(blk_e, nact, xin, w1_all, w3_all, w2_all, layer):
    nb = xin.shape[0] // (EXPERT_BLOCK * ROW_CHUNKS)
    tile = (EXPERT_BLOCK * ROW_CHUNKS, LANES)
    weight = lambda r, c: pl.BlockSpec((None, None, r, c), lambda i, blk, na: (layer, blk[i], 0, 0))
    return pl.pallas_call(
        _expert_kernel,
        grid_spec=pltpu.PrefetchScalarGridSpec(
            num_scalar_prefetch=2,
            grid=(nb,),
            in_specs=[pl.BlockSpec(tile, lambda i, blk, na: (jnp.minimum(i, na[0] - 1), 0)),
                      weight(D_MODEL, EXPERT_HIDDEN), weight(D_MODEL, EXPERT_HIDDEN), weight(EXPERT_HIDDEN, D_MODEL)],
            out_specs=pl.BlockSpec(tile, lambda i, blk, na: (i, 0)),
            scratch_shapes=[pltpu.VMEM((D_MODEL, EXPERT_HIDDEN), BF16), pltpu.VMEM((D_MODEL, EXPERT_HIDDEN), BF16),
                            pltpu.VMEM((EXPERT_HIDDEN, D_MODEL), BF16)]),
        out_shape=jax.ShapeDtypeStruct(xin.shape, F32),
        compiler_params=_cparams(),
        name="moe_experts",
    )(blk_e, nact, xin, w1_all, w3_all, w2_all)


COMBINE_TILE = 256


def _combine_kernel(dest_ref, y_ref, meta_ref, h_ref, g2_ref, lng_ref, lnb_ref, o_ref, buf, sem, *, alpha):
    t = COMBINE_TILE
    i = pl.program_id(0)
    last = pl.num_programs(0) - 1

    def copy(step, half, k, slot):
        src = _row_tile(y_ref, dest_ref[slot * (dest_ref.shape[0] // TOP_K) + step * t + k])
        return pltpu.make_async_copy(src, _row_tile(buf.at[half, slot], k), sem.at[half])

    def issue(step, half):
        def body(k, carry):
            for slot in range(TOP_K):
                copy(step, half, k, slot).start(priority=slot)
            return carry
        lax.fori_loop(0, t, body, 0, unroll=DMA_LOOP_UNROLL)

    @pl.when(i == 0)
    def _():
        issue(0, 0)

    @pl.when(i < last)
    def _():
        issue(i + 1, (i + 1) % 2)

    half = i % 2

    def drain(k, carry):
        for slot in range(TOP_K):
            copy(i, half, k, slot).wait()
        return carry

    lax.fori_loop(0, t, drain, 0, unroll=DMA_LOOP_UNROLL)
    meta = meta_ref[...]
    f = meta[:, 2:3] * _rows_from_tiles(buf.at[half, 0]) + meta[:, 3:4] * _rows_from_tiles(buf.at[half, 1])
    o_ref[...] = _layer_norm(alpha * h_ref[...] + g2_ref[...] * f) * lng_ref[...] + lnb_ref[...]


def _combine(dest_flat, yout, meta, h1, mods, ln_g, ln_b, tps, alpha):
    n = h1.shape[0]
    t = COMBINE_TILE
    d = D_MODEL

    def mod_map(i, dest):
        return (jnp.where(i % tps == 0, 8, i // tps), 0, 5)

    return pl.pallas_call(
        functools.partial(_combine_kernel, alpha=alpha),
        grid_spec=pltpu.PrefetchScalarGridSpec(
            num_scalar_prefetch=1,
            grid=(n // t,),
            in_specs=[pl.BlockSpec(memory_space=pl.ANY),
                      pl.BlockSpec((t, LANES), lambda i, dest: (i, 0)),
                      pl.BlockSpec((t, d), lambda i, dest: (i, 0)),
                      pl.BlockSpec((None, 1, d), mod_map),
                      pl.BlockSpec((1, d), lambda i, dest: (0, 0)),
                      pl.BlockSpec((1, d), lambda i, dest: (0, 0))],
            out_specs=pl.BlockSpec((t, d), lambda i, dest: (i, 0)),
            scratch_shapes=[pltpu.VMEM((2, TOP_K, t * ROW_CHUNKS, LANES), F32), pltpu.SemaphoreType.DMA((2,))]),
        out_shape=jax.ShapeDtypeStruct((n, d), F32),
        compiler_params=_cparams(dimension_semantics=("arbitrary",)),
        name="moe_combine",
    )(dest_flat, yout, meta, h1, mods, ln_g, ln_b)


def _prep_w_in(w_in):
    cq, ckv, kr, z, xbc, dt, lx, lg, gate = _split_sections(w_in)
    zeros = lambda w: jnp.zeros(w_in.shape[:-1] + (w,), w_in.dtype)
    out = jnp.concatenate([z, lx, lg, gate, xbc, cq, ckv, kr, zeros(64),
                           dt[..., :SSD_HEADS], zeros(DT_DIR_STRIDE - SSD_HEADS),
                           dt[..., SSD_HEADS:], zeros(DT_DIR_STRIDE - SSD_HEADS)], axis=-1)
    return out.astype(BF16)


def _split_sections(w):
    sizes = (Q_LORA, KV_LORA, QK_ROPE, SSD_INNER, SSD_INNER + 2 * SSD_GROUPS * SSD_STATE, 2 * SSD_HEADS,
             LRU_WIDTH, LRU_WIDTH, N_BRANCH * D_MODEL)
    out, start = [], 0
    for size in sizes:
        out.append(w[..., start:start + size])
        start += size
    return out


def _dir_row(v):
    row = jnp.zeros((LANES,), F32)
    row = row.at[0:SSD_HEADS].set(v[0]).at[DT_DIR_STRIDE:DT_DIR_STRIDE + SSD_HEADS].set(v[1])
    return row[None, :]


def _expand_matrix(rev):
    rows = jnp.arange(LANES)[:, None]
    cols = jnp.arange(SSD_INNER)[None, :]
    base = DT_DIR_STRIDE if rev else 0
    return (rows - base == cols // SSD_HEADDIM).astype(BF16)


def _block_diag_gates(wa, wx):
    per = LRU_GROUP // LRU_BW
    eye = jnp.eye(per, dtype=wa.dtype)

    def bd(w):
        w = w.reshape(LRU_WIDTH // LRU_GROUP, per, LRU_BW, LRU_BW)
        return jnp.einsum('gicd,ij->gicjd', w, eye).reshape(LRU_WIDTH // LRU_GROUP, LRU_GROUP, LRU_GROUP)

    return jnp.concatenate([bd(wa), bd(wx)], axis=-1).astype(BF16)


def _rope_tables(n_ctx, seq):
    rows = seq // GRID_W
    row_pos = jnp.repeat(jnp.arange(rows, dtype=F32), GRID_W)
    col_pos = (jnp.arange(rows * GRID_W) % GRID_W).astype(F32)
    inv_freq = ROPE_THETA ** (-jnp.arange(ROPE_FREQS, dtype=F32) / ROPE_FREQS)
    ang = [row_pos[:, None] * inv_freq, col_pos[:, None] * inv_freq]
    cos = jnp.concatenate([jnp.cos(ang[0]), jnp.cos(ang[0]), jnp.cos(ang[1]), jnp.cos(ang[1])], axis=1)
    sin = jnp.concatenate([-jnp.sin(ang[0]), jnp.sin(ang[0]), -jnp.sin(ang[1]), jnp.sin(ang[1])], axis=1)
    pad = lambda t, fill: jnp.concatenate(
        [jnp.concatenate([jnp.full((n_ctx, QK_ROPE), fill, F32), t], axis=0),
         jnp.zeros((n_ctx + seq, LANES - QK_ROPE), F32)], axis=1)
    return pad(cos, 1.0), pad(sin, 0.0)


def kernel(x, c, ctx, c_ctx, w_mod, b_mod, w_in, q_norm_w, kv_norm_w, w_uq, w_ukv, ssd_conv_w, ssd_conv_b, ssd_a_log, ssd_dt_bias, ssd_d, ssd_norm_w, lru_conv_w, lru_conv_b, lru_wa, lru_ba, lru_wx, lru_bx, lru_lambda, w_branch, w_out, ln1_g, ln1_b, ln2_g, ln2_b, router_wg, router_bg, router_we, router_be, exp_w1, exp_w3, exp_w2):
    bsz, seq, d = x.shape
    n_ctx = ctx.shape[1]
    depth = w_mod.shape[0]
    assert d == D_MODEL and n_ctx == ROW_TILE and seq % ROW_TILE == 0 and bsz == SUBLANES
    s_all = n_ctx + seq
    tps = s_all // ROW_TILE
    n = bsz * s_all
    alpha = (2 * depth) ** 0.25

    h = jnp.concatenate([ctx, x], axis=1).reshape(n, d)
    cvec = jnp.zeros((16, d), F32).at[:bsz].set(c).at[bsz].set(c_ctx)
    mods_all = _mod_vectors(cvec, w_mod, b_mod).reshape(depth, 16, 1, 6 * d)
    cos_t, sin_t = _rope_tables(n_ctx, seq)
    expand = (_expand_matrix(False), _expand_matrix(True))

    w_in_p = _prep_w_in(w_in)
    wq = jnp.pad(w_uq.reshape(depth, Q_LORA, MLA_HEADS, QK_NOPE + QK_ROPE),
                 ((0, 0), (0, 0), (0, 0), (0, QK_PAD - QK_NOPE - QK_ROPE))).reshape(depth, Q_LORA, -1).astype(BF16)
    wkv4 = w_ukv.reshape(depth, KV_LORA, MLA_HEADS, QK_NOPE + V_DIM)
    wkv = jnp.concatenate([wkv4[..., :QK_NOPE].reshape(depth, KV_LORA, -1),
                           wkv4[..., QK_NOPE:].reshape(depth, KV_LORA, -1)], axis=-1).astype(BF16)
    w_router = jnp.concatenate([router_wg, router_we,
                                jnp.zeros((depth, d, LANES - N_GROUPS - N_EXPERTS), F32)], axis=-1)
    b_router = jnp.concatenate([router_bg, router_be,
                                jnp.zeros((depth, LANES - N_GROUPS - N_EXPERTS), F32)], axis=-1)
    wb_all = w_branch.astype(BF16)
    wo_all = w_out.astype(BF16)

    n_blocks = -(-(n * TOP_K) // EXPERT_BLOCK) + N_EXPERTS
    row1 = lambda v: v.reshape(1, -1)

    for l in range(depth):
        mods = mods_all[l]
        proj, misc = _in_proj(h, mods, w_in_p, l, tps)
        proj3 = proj.reshape(bsz, s_all, -1)
        misc3 = misc.reshape(bsz, s_all, -1)

        q, k, vt = _mla_prep(proj, misc, row1(q_norm_w[l]), row1(kv_norm_w[l]), wq, wkv, l, cos_t, sin_t, tps)
        att = _attention(q.reshape(bsz, s_all, -1), k.reshape(bsz, s_all, -1), vt, n_ctx)

        xbc_conv = _dwconv(proj3, COL_XBC, 2 * SSD_INNER, ssd_conv_w[l], row1(ssd_conv_b[l]), True)
        dtb = _dir_row(ssd_dt_bias[l])
        aneg = _dir_row(-jnp.exp(ssd_a_log[l]))
        ys = _ssd_scan(xbc_conv, misc3, dtb, aneg, expand, n_ctx)

        lx_conv = _dwconv(proj3, COL_LX, LRU_WIDTH, lru_conv_w[l], row1(lru_conv_b[l]), False)
        hs = [_lru_scan(lx_conv, _block_diag_gates(lru_wa[l, dr], lru_wx[l, dr]), row1(lru_ba[l, dr]),
                        row1(lru_bx[l, dr]), row1(lru_lambda[l, dr]), n_ctx, bool(dr)) for dr in range(2)]

        d_skip_row = row1(jnp.repeat(ssd_d[l], SSD_HEADDIM))
        h1, u2t = _merge(att.reshape(n, -1), ys[0].reshape(n, -1), ys[1].reshape(n, -1),
                         xbc_conv.reshape(n, -1), proj, hs[0].reshape(n, -1), hs[1].reshape(n, -1), h,
                         d_skip_row, row1(ssd_norm_w[l]), wb_all, wo_all, l, mods,
                         row1(ln1_g[l]), row1(ln1_b[l]), tps, alpha)

        meta, meta_t, cnt = _router(u2t, w_router[l], row1(b_router[l]))
        counts = cnt[0, :N_EXPERTS].astype(jnp.int32)
        padded = (counts + EXPERT_BLOCK - 1) // EXPERT_BLOCK * EXPERT_BLOCK
        pend = jnp.cumsum(padded)
        pstart = pend - padded
        eid = meta_t[0:TOP_K].astype(jnp.int32)
        rank = meta_t[4:4 + TOP_K].astype(jnp.int32)
        seg_start = jnp.zeros_like(eid)
        for e in range(N_EXPERTS):
            seg_start = seg_start + jnp.where(eid == e, pstart[e], 0)
        dest = (seg_start + rank).reshape(-1)
        blk_start = jnp.arange(n_blocks, dtype=jnp.int32) * EXPERT_BLOCK
        blk_e = jnp.minimum(jnp.sum(pend[None, :] <= blk_start[:, None], axis=1), N_EXPERTS - 1).astype(jnp.int32)
        nact = (pend[-1:] // EXPERT_BLOCK).astype(jnp.int32)
        blk_e = jnp.where(jnp.arange(n_blocks) < nact[0], blk_e, jnp.take(blk_e, jnp.maximum(nact[0] - 1, 0)))

        xin = _dispatch(dest, pend.astype(jnp.int32), u2t, n_blocks * EXPERT_BLOCK)
        yout = _experts(blk_e, nact, xin, exp_w1, exp_w3, exp_w2, l)
        h = _combine(dest, yout, meta, h1, mods, row1(ln2_g[l]), row1(ln2_b[l]), tps, alpha)

    return h.reshape(bsz, s_all, d)[:, n_ctx:, :]
```
